```python
import jax, jax.numpy as jnp
from jax import lax
import numpy as np

D_MODEL = 1024
BATCH = 4
SEQ = 4096
DEPTH = 4

HEAD_DIM = 64
N_Q_HEADS = 8
N_KV_HEADS = 2
GQA_GROUP = N_Q_HEADS // N_KV_HEADS
ATTN_WIDTH = N_Q_HEADS * HEAD_DIM
KV_WIDTH = N_KV_HEADS * HEAD_DIM
WINDOW = 128
BLOCK = 128
CONV_WIDTH = D_MODEL - ATTN_WIDTH
CONV_KERNEL = 31
IN_PROJ_WIDTH = ATTN_WIDTH + 2 * KV_WIDTH + 2 * CONV_WIDTH
D_FF_DENSE = 2816
N_EXPERTS = 8
TOP_K = 2
D_FF_EXPERT = 3584
N_DENSE = (DEPTH + 1) // 2
N_MOE = DEPTH // 2
NORM_EPS = 1e-6

kernel_name = "hybrid_swa_sink_conformer_moe"


def rms_norm(x, g):
    xf = x.astype(jnp.float32)
    y = xf * lax.rsqrt(jnp.mean(xf * xf, axis=-1, keepdims=True) + NORM_EPS)
    return (y * g.astype(jnp.float32)).astype(x.dtype)


def layer_norm(x, g, b):
    xf = x.astype(jnp.float32)
    mu = jnp.mean(xf, axis=-1, keepdims=True)
    var = jnp.mean(jnp.square(xf - mu), axis=-1, keepdims=True)
    y = (xf - mu) * lax.rsqrt(var + NORM_EPS)
    return (y * g.astype(jnp.float32) + b.astype(jnp.float32)).astype(x.dtype)


def alibi_slopes():
    h = jnp.arange(1, N_Q_HEADS + 1, dtype=jnp.float32)
    return jnp.exp2(-8.0 * h / N_Q_HEADS)


def sliding_window_attention(q, k, v, sinks):
    B, S = q.shape[0], q.shape[1]
    nb = S // BLOCK
    qb = q.reshape(B, nb, BLOCK, N_KV_HEADS, GQA_GROUP, HEAD_DIM)

    def band(t):
        t = t.reshape(B, nb, BLOCK, N_KV_HEADS, HEAD_DIM)
        prev = jnp.pad(t, ((0, 0), (1, 0), (0, 0), (0, 0), (0, 0)))[:, :-1]
        return jnp.concatenate([prev, t], axis=2)

    kb, vb = band(k), band(v)
    scores = jnp.einsum('bnqkgd,bnskd->bnkgqs', qb, kb,
                        preferred_element_type=jnp.float32) * (HEAD_DIM ** -0.5)
    dist = (jnp.arange(BLOCK)[:, None] + BLOCK) - jnp.arange(2 * BLOCK)[None, :]
    valid = (dist >= 0) & (dist < WINDOW)
    first_blk_pad = (jnp.arange(nb)[:, None, None] == 0) & (jnp.arange(2 * BLOCK) < BLOCK)[None, None, :]
    valid = valid[None] & ~first_blk_pad
    slopes = alibi_slopes().reshape(N_KV_HEADS, GQA_GROUP)
    scores = scores - slopes[:, :, None, None] * dist.astype(jnp.float32)
    scores = jnp.where(valid[None, :, None, None], scores, -jnp.inf)
    sink = jnp.broadcast_to(sinks.astype(jnp.float32).reshape(N_KV_HEADS, GQA_GROUP, 1, 1),
                            scores.shape[:-1] + (1,))
    probs = jax.nn.softmax(jnp.concatenate([scores, sink], axis=-1), axis=-1)[..., :-1]
    out = jnp.einsum('bnkgqs,bnskd->bnqkgd', probs.astype(v.dtype), vb)
    return out.reshape(B, S, ATTN_WIDTH)


def conformer_conv(u, gate, w_dw, b_dw, ln_g, ln_b):
    h = u * jax.nn.sigmoid(gate)
    hp = jnp.pad(h, ((0, 0), (CONV_KERNEL - 1, 0), (0, 0)))
    h = lax.conv_general_dilated(hp, w_dw[:, None, :].astype(h.dtype), window_strides=(1,),
                                 padding='VALID', dimension_numbers=('NWC', 'WIO', 'NWC'),
                                 feature_group_count=CONV_WIDTH) + b_dw
    h = layer_norm(h, ln_g, ln_b)
    return jax.nn.silu(h)


def dense_swiglu(h, w_gate, w_up, w_down):
    return (jax.nn.silu(h @ w_gate) * (h @ w_up)) @ w_down


def moe_swiglu(h, w_router, w_gate, w_up, w_down):
    B, S, D = h.shape
    t = h.reshape(-1, D)
    logits = (t @ w_router).astype(jnp.float32)
    top_v, top_i = lax.top_k(logits, TOP_K)
    top_w = jax.nn.softmax(top_v, axis=-1)
    combine = jnp.sum(jax.nn.one_hot(top_i, N_EXPERTS, dtype=jnp.float32) * top_w[..., None], axis=1)
    out = jnp.zeros_like(t)
    for e in range(N_EXPERTS):
        y = (jax.nn.silu(t @ w_gate[e]) * (t @ w_up[e])) @ w_down[e]
        out = out + combine[:, e:e + 1].astype(t.dtype) * y
    return out.reshape(B, S, D)


def setup_inputs(seed: int = 0) -> dict:
    key = jax.random.key(seed)
    ks = jax.random.split(key, 24)
    f32 = jnp.float32
    nrm = lambda k, shape, scale: jax.random.normal(k, shape, f32) * scale
    return {
        "x": nrm(ks[0], (BATCH, SEQ, D_MODEL), 1.0),
        "attn_norm_g": 1.0 + nrm(ks[1], (DEPTH, D_MODEL), 0.02),
        "w_in": nrm(ks[2], (DEPTH, D_MODEL, IN_PROJ_WIDTH), D_MODEL ** -0.5),
        "q_norm_g": 1.0 + nrm(ks[3], (DEPTH, HEAD_DIM), 0.02),
        "k_norm_g": 1.0 + nrm(ks[4], (DEPTH, HEAD_DIM), 0.02),
        "sinks": nrm(ks[5], (DEPTH, N_Q_HEADS), 0.5),
        "conv_w": nrm(ks[6], (DEPTH, CONV_KERNEL, CONV_WIDTH), CONV_KERNEL ** -0.5),
        "conv_b": nrm(ks[7], (DEPTH, CONV_WIDTH), 0.02),
        "conv_ln_g": 1.0 + nrm(ks[8], (DEPTH, CONV_WIDTH), 0.02),
        "conv_ln_b": nrm(ks[9], (DEPTH, CONV_WIDTH), 0.02),
        "w_out": nrm(ks[10], (DEPTH, D_MODEL, D_MODEL), D_MODEL ** -0.5),
        "ffn_norm_g": 1.0 + nrm(ks[11], (DEPTH, D_MODEL), 0.02),
        "dense_w_gate": nrm(ks[12], (N_DENSE, D_MODEL, D_FF_DENSE), D_MODEL ** -0.5),
        "dense_w_up": nrm(ks[13], (N_DENSE, D_MODEL, D_FF_DENSE), D_MODEL ** -0.5),
        "dense_w_down": nrm(ks[14], (N_DENSE, D_FF_DENSE, D_MODEL), D_FF_DENSE ** -0.5),
        "w_router": nrm(ks[15], (N_MOE, D_MODEL, N_EXPERTS), D_MODEL ** -0.5),
        "moe_w_gate": nrm(ks[16], (N_MOE, N_EXPERTS, D_MODEL, D_FF_EXPERT), D_MODEL ** -0.5),
        "moe_w_up": nrm(ks[17], (N_MOE, N_EXPERTS, D_MODEL, D_FF_EXPERT), D_MODEL ** -0.5),
        "moe_w_down": nrm(ks[18], (N_MOE, N_EXPERTS, D_FF_EXPERT, D_MODEL), D_FF_EXPERT ** -0.5),
    }


def reference(x, attn_norm_g, w_in, q_norm_g, k_norm_g, sinks, conv_w, conv_b, conv_ln_g,
              conv_ln_b, w_out, ffn_norm_g, dense_w_gate, dense_w_up, dense_w_down,
              w_router, moe_w_gate, moe_w_up, moe_w_down):
    B, S, _ = x.shape
    o_k = ATTN_WIDTH
    o_v = o_k + KV_WIDTH
    o_c = o_v + KV_WIDTH
    o_g = o_c + CONV_WIDTH
    for layer in range(DEPTH):
        h = rms_norm(x, attn_norm_g[layer])
        proj = h @ w_in[layer]
        q = proj[..., :o_k].reshape(B, S, N_Q_HEADS, HEAD_DIM)
        k = proj[..., o_k:o_v].reshape(B, S, N_KV_HEADS, HEAD_DIM)
        v = proj[..., o_v:o_c].reshape(B, S, N_KV_HEADS, HEAD_DIM)
        q = rms_norm(q, q_norm_g[layer])
        k = rms_norm(k, k_norm_g[layer])
        attn_out = sliding_window_attention(q, k, v, sinks[layer])
        conv_out = conformer_conv(proj[..., o_c:o_g], proj[..., o_g:], conv_w[layer],
                                  conv_b[layer], conv_ln_g[layer], conv_ln_b[layer])
        x = x + jnp.concatenate([attn_out, conv_out], axis=-1) @ w_out[layer]
        h = rms_norm(x, ffn_norm_g[layer])
        if layer % 2 == 0:
            i = layer // 2
            x = x + dense_swiglu(h, dense_w_gate[i], dense_w_up[i], dense_w_down[i])
        else:
            i = layer // 2
            x = x + moe_swiglu(h, w_router[i], moe_w_gate[i], moe_w_up[i], moe_w_down[i])
    return x
```

```python
import functools

import jax
import jax.numpy as jnp
from jax import lax
from jax.experimental import pallas as pl
from jax.experimental.pallas import tpu as pltpu

F32 = jnp.float32
BF16 = jnp.bfloat16

D_MODEL = 1024
HEAD_DIM = 64
N_Q_HEADS = 8
N_KV_HEADS = 2
ATTN_WIDTH = N_Q_HEADS * HEAD_DIM
KV_WIDTH = N_KV_HEADS * HEAD_DIM
QK_WIDTH = ATTN_WIDTH + KV_WIDTH
BLOCK = 128
CONV_WIDTH = D_MODEL - ATTN_WIDTH
CONV_KERNEL = 31
IN_PROJ_WIDTH = ATTN_WIDTH + 2 * KV_WIDTH + 2 * CONV_WIDTH
N_EXPERTS = 8
NORM_EPS = 1e-6

V7X_LANES = 128
V7X_SUBLANES = 8
V7X_VMEM_LIMIT_BYTES = 56 * 1024 * 1024
ROW_CHUNKS = D_MODEL // V7X_LANES

TM_PROJ = 512
TC_CONV = 256
CONV_HALO = 32
CONV_ROWS = 32
TM_FFN = 512
TM_MOE = 512
TM_TOK = 512


def _params(sem, vmem=V7X_VMEM_LIMIT_BYTES):
    return pltpu.CompilerParams(dimension_semantics=sem, vmem_limit_bytes=vmem)


def _sigmoid(x):
    return 1.0 / (1.0 + jnp.exp(-x))


def _rms_norm_bf16(x, g):
    ms = jnp.mean(x * x, axis=-1, keepdims=True)
    return (x * lax.rsqrt(ms + NORM_EPS) * g).astype(BF16)


def _mixer_in_kernel(x_ref, g_ref, w_ref, gain_ref, seg_ref, q_ref, k_ref, v_ref, glu_ref):
    h = _rms_norm_bf16(x_ref[...], g_ref[...])
    proj = jnp.dot(h, w_ref[...], preferred_element_type=F32)
    qk = proj[:, :QK_WIDTH]
    sq = qk * qk
    hi = sq.astype(BF16)
    lo = (sq - hi.astype(F32)).astype(BF16)
    seg = seg_ref[...]
    ss = (jnp.dot(hi, seg, preferred_element_type=F32)
          + jnp.dot(lo, seg, preferred_element_type=F32))
    qkn = qk * lax.rsqrt(ss * (1.0 / HEAD_DIM) + NORM_EPS) * gain_ref[...]
    q_ref[...] = qkn[:, :ATTN_WIDTH].astype(BF16)
    lane = lax.broadcasted_iota(jnp.int32, (x_ref.shape[0], KV_WIDTH), 1)
    low = lane < HEAD_DIM
    for src, dst in ((qkn[:, ATTN_WIDTH:QK_WIDTH], k_ref),
                     (proj[:, QK_WIDTH:QK_WIDTH + KV_WIDTH], v_ref)):
        rot = pltpu.roll(src, HEAD_DIM, axis=1)
        dst[:, :KV_WIDTH] = jnp.where(low, src, rot).astype(BF16)
        dst[:, KV_WIDTH:] = jnp.where(low, rot, src).astype(BF16)
    o_c = QK_WIDTH + KV_WIDTH
    u = proj[:, o_c:o_c + CONV_WIDTH]
    gate = proj[:, o_c + CONV_WIDTH:]
    glu_ref[...] = u * _sigmoid(gate)


def _mixer_in(x, g, w_bf16, gain, seg):
    t = x.shape[0]
    tm = TM_PROJ
    row = lambda i: (i, 0)
    fixed = lambda i: (0, 0)
    return pl.pallas_call(
        _mixer_in_kernel,
        grid=(t // tm,),
        in_specs=[
            pl.BlockSpec((tm, D_MODEL), row),
            pl.BlockSpec((1, D_MODEL), fixed),
            pl.BlockSpec((D_MODEL, IN_PROJ_WIDTH), fixed),
            pl.BlockSpec((1, QK_WIDTH), fixed),
            pl.BlockSpec((QK_WIDTH, QK_WIDTH), fixed),
        ],
        out_specs=[
            pl.BlockSpec((tm, ATTN_WIDTH), row),
            pl.BlockSpec((tm, 2 * KV_WIDTH), row),
            pl.BlockSpec((tm, 2 * KV_WIDTH), row),
            pl.BlockSpec((tm, CONV_WIDTH), row),
        ],
        out_shape=[
            jax.ShapeDtypeStruct((t, ATTN_WIDTH), BF16),
            jax.ShapeDtypeStruct((t, 2 * KV_WIDTH), BF16),
            jax.ShapeDtypeStruct((t, 2 * KV_WIDTH), BF16),
            jax.ShapeDtypeStruct((t, CONV_WIDTH), F32),
        ],
        compiler_params=_params(("arbitrary",)),
        name="mixer_in",
    )(x, g, w_bf16, gain, seg)


def _attn_kernel(sink_ref, q_ref, kc_ref, kp_ref, vc_ref, vp_ref, bias_ref, o_ref):
    low = lax.broadcasted_iota(jnp.int32, (2 * BLOCK, 2 * HEAD_DIM), 1) < HEAD_DIM
    low_q = lax.broadcasted_iota(jnp.int32, (BLOCK, 2 * HEAD_DIM), 1) < HEAD_DIM
    zero = jnp.zeros((), BF16)
    contract_last = (((1,), (1,)), ((), ()))
    for g in range(N_KV_HEADS):
        cols = slice(g * 2 * HEAD_DIM, (g + 1) * 2 * HEAD_DIM)
        kd = jnp.concatenate([kp_ref[:, cols], kc_ref[:, cols]], axis=0)
        vd = jnp.concatenate([vp_ref[:, cols], vc_ref[:, cols]], axis=0)
        k_lo = jnp.where(low, kd, zero)
        k_hi = jnp.where(low, zero, kd)
        pair0 = slice((2 * g) * 2 * HEAD_DIM, (2 * g + 1) * 2 * HEAD_DIM)
        pair1 = slice((2 * g + 1) * 2 * HEAD_DIM, (2 * g + 2) * 2 * HEAD_DIM)
        q2 = jnp.concatenate([q_ref[:, pair0], q_ref[:, pair1]], axis=0)
        s_lo = lax.dot_general(q2, k_lo, contract_last, preferred_element_type=F32)
        s_hi = lax.dot_general(q2, k_hi, contract_last, preferred_element_type=F32)
        outs = {}
        for s, first_head in ((s_lo, 4 * g), (s_hi, 4 * g + 1)):
            for half in range(2):
                head = first_head + 2 * half
                sh = s[half * BLOCK:(half + 1) * BLOCK] + bias_ref[0, head]
                sink = sink_ref[head]
                m = jnp.maximum(jnp.max(sh, axis=-1, keepdims=True), sink)
                p = jnp.exp(sh - m)
                den = jnp.sum(p, axis=-1, keepdims=True) + jnp.exp(sink - m)
                pv = jnp.dot(p.astype(BF16), vd, preferred_element_type=F32)
                outs[head] = pv / den
        o_ref[:, pair0] = jnp.where(low_q, outs[4 * g], outs[4 * g + 1]).astype(BF16)
        o_ref[:, pair1] = jnp.where(low_q, outs[4 * g + 2], outs[4 * g + 3]).astype(BF16)


def _attention(q, kd, vd, sinks, bias, batch, seq):
    t = q.shape[0]
    nb = seq // BLOCK
    cur = lambda b, n: (b * nb + n, 0)
    prev = lambda b, n: (b * nb + jnp.maximum(n - 1, 0), 0)
    return pl.pallas_call(
        _attn_kernel,
        grid=(batch, nb),
        in_specs=[
            pl.BlockSpec(memory_space=pltpu.SMEM),
            pl.BlockSpec((BLOCK, ATTN_WIDTH), cur),
            pl.BlockSpec((BLOCK, 2 * KV_WIDTH), cur),
            pl.BlockSpec((BLOCK, 2 * KV_WIDTH), prev),
            pl.BlockSpec((BLOCK, 2 * KV_WIDTH), cur),
            pl.BlockSpec((BLOCK, 2 * KV_WIDTH), prev),
            pl.BlockSpec((1, N_Q_HEADS, BLOCK, 2 * BLOCK),
                         lambda b, n: (jnp.minimum(n, 1), 0, 0, 0)),
        ],
        out_specs=pl.BlockSpec((BLOCK, ATTN_WIDTH), cur),
        out_shape=jax.ShapeDtypeStruct((t, ATTN_WIDTH), BF16),
        compiler_params=_params(("arbitrary", "arbitrary")),
        name="attention",
    )(sinks, q, kd, kd, vd, vd, bias)


def _attention_bias():
    heads = jnp.arange(1, N_Q_HEADS + 1, dtype=F32)
    slopes = jnp.exp2(-8.0 * heads / N_Q_HEADS)
    dist = (jnp.arange(BLOCK)[:, None] + BLOCK) - jnp.arange(2 * BLOCK)[None, :]
    valid = (dist >= 0) & (dist < BLOCK)
    has_prev = jnp.arange(2 * BLOCK)[None, :] >= BLOCK
    valid = jnp.stack([valid & has_prev, valid])
    bias = -slopes[:, None, None] * dist.astype(F32)
    return jnp.where(valid[:, None], bias[None], -jnp.inf)


def _conv_kernel(tiles_per_seq, cur_ref, halo_ref, w_ref, b_ref, lng_ref, lnb_ref, o_ref, buf):
    i = pl.program_id(0)
    first = (i % tiles_per_seq) == 0
    tc = cur_ref.shape[0]
    buf[:CONV_HALO] = jnp.where(first, 0.0, halo_ref[...])
    buf[CONV_HALO:] = cur_ref[...]
    base = CONV_HALO - (CONV_KERNEL - 1)
    bias = b_ref[...]
    lng = lng_ref[...]
    lnb = lnb_ref[...]
    groups = CONV_ROWS // V7X_SUBLANES
    for r0 in range(0, tc, CONV_ROWS):
        acc = [None] * groups
        for k in range(CONV_KERNEL):
            wk = w_ref[k]
            for gi in range(groups):
                off = base + r0 + gi * V7X_SUBLANES + k
                term = buf[off:off + V7X_SUBLANES] * wk
                acc[gi] = term if acc[gi] is None else acc[gi] + term
        y = jnp.concatenate(acc, axis=0) + bias
        mu = jnp.mean(y, axis=-1, keepdims=True)
        yc = y - mu
        var = jnp.mean(yc * yc, axis=-1, keepdims=True)
        z = yc * lax.rsqrt(var + NORM_EPS) * lng + lnb
        o_ref[r0:r0 + CONV_ROWS] = (z * _sigmoid(z)).astype(BF16)


def _conv(glu, w_b, b, lng, lnb, seq):
    t = glu.shape[0]
    tc = TC_CONV
    ratio = tc // CONV_HALO
    fixed = lambda i: (0, 0)
    return pl.pallas_call(
        functools.partial(_conv_kernel, seq // tc),
        grid=(t // tc,),
        in_specs=[
            pl.BlockSpec((tc, CONV_WIDTH), lambda i: (i, 0)),
            pl.BlockSpec((CONV_HALO, CONV_WIDTH), lambda i: (jnp.maximum(i * ratio - 1, 0), 0)),
            pl.BlockSpec((CONV_KERNEL, V7X_SUBLANES, CONV_WIDTH), lambda i: (0, 0, 0)),
            pl.BlockSpec((1, CONV_WIDTH), fixed),
            pl.BlockSpec((1, CONV_WIDTH), fixed),
            pl.BlockSpec((1, CONV_WIDTH), fixed),
        ],
        out_specs=pl.BlockSpec((tc, CONV_WIDTH), lambda i: (i, 0)),
        out_shape=jax.ShapeDtypeStruct((t, CONV_WIDTH), BF16),
        scratch_shapes=[pltpu.VMEM((CONV_HALO + tc, CONV_WIDTH), F32)],
        compiler_params=_params(("arbitrary",)),
        name="conv",
    )(glu, glu, w_b, b, lng, lnb)


def _mixer_out_kernel(attn_ref, conv_ref, x_ref, w_ref, o_ref):
    w = w_ref[...]
    y = (jnp.dot(attn_ref[...], w[:ATTN_WIDTH], preferred_element_type=F32)
         + jnp.dot(conv_ref[...], w[ATTN_WIDTH:], preferred_element_type=F32))
    o_ref[...] = x_ref[...] + y


def _mixer_out_router_kernel(attn_ref, conv_ref, x_ref, w_ref, g_ref, wr_hi_ref, wr_lo_ref,
                             o_ref, route_ref, count_ref, carry):
    i = pl.program_id(0)
    tm = x_ref.shape[0]

    @pl.when(i == 0)
    def _():
        carry[...] = jnp.zeros_like(carry)

    w = w_ref[...]
    y = (jnp.dot(attn_ref[...], w[:ATTN_WIDTH], preferred_element_type=F32)
         + jnp.dot(conv_ref[...], w[ATTN_WIDTH:], preferred_element_type=F32))
    x = x_ref[...] + y
    o_ref[...] = x

    ms = jnp.mean(x * x, axis=-1, keepdims=True)
    h = x * lax.rsqrt(ms + NORM_EPS) * g_ref[...]
    h_hi = h.astype(BF16)
    h_lo = (h - h_hi.astype(F32)).astype(BF16)
    wr_hi = wr_hi_ref[...]
    logits = (jnp.dot(h_hi, wr_hi, preferred_element_type=F32)
              + jnp.dot(h_lo, wr_hi, preferred_element_type=F32)
              + jnp.dot(h_hi, wr_lo_ref[...], preferred_element_type=F32))

    lane = lax.broadcasted_iota(jnp.int32, (tm, V7X_LANES), 1)
    lane_f = lane.astype(F32)
    neg_inf = -jnp.inf
    lg = jnp.where(lane < N_EXPERTS, logits, neg_inf)
    m1 = jnp.max(lg, axis=-1, keepdims=True)
    i1 = jnp.min(jnp.where(lg == m1, lane_f, float(V7X_LANES)), axis=-1, keepdims=True)
    oh1 = lane_f == i1
    lg2 = jnp.where(oh1, neg_inf, lg)
    m2 = jnp.max(lg2, axis=-1, keepdims=True)
    i2 = jnp.min(jnp.where(lg2 == m2, lane_f, float(V7X_LANES)), axis=-1, keepdims=True)
    oh2 = lane_f == i2
    e = jnp.exp(m2 - m1)
    w1 = 1.0 / (1.0 + e)
    w2 = e / (1.0 + e)

    member = jnp.where(oh1 | oh2, 1.0, 0.0)
    r_id = lax.broadcasted_iota(jnp.int32, (tm, tm), 0)
    c_id = lax.broadcasted_iota(jnp.int32, (tm, tm), 1)
    tri = jnp.where(r_id > c_id, 1.0, 0.0).astype(BF16)
    before = jnp.dot(tri, member.astype(BF16), preferred_element_type=F32) + carry[...]
    rank1 = jnp.sum(jnp.where(oh1, before, 0.0), axis=-1, keepdims=True)
    rank2 = jnp.sum(jnp.where(oh2, before, 0.0), axis=-1, keepdims=True)
    carry[...] = carry[...] + jnp.sum(member, axis=0, keepdims=True)

    route = jnp.where(lane == 0, i1, 0.0)
    route = jnp.where(lane == 1, i2, route)
    route = jnp.where(lane == 2, w1, route)
    route = jnp.where(lane == 3, w2, route)
    route = jnp.where(lane == 4, rank1, route)
    route = jnp.where(lane == 5, rank2, route)
    route_ref[...] = route
    count_ref[...] = jnp.broadcast_to(carry[...], count_ref.shape)


def _mixer_out(attn, conv, x, w_bf16, router=None):
    t = x.shape[0]
    tm = TM_PROJ
    row = lambda i: (i, 0)
    fixed = lambda i: (0, 0)
    in_specs = [
        pl.BlockSpec((tm, ATTN_WIDTH), row),
        pl.BlockSpec((tm, CONV_WIDTH), row),
        pl.BlockSpec((tm, D_MODEL), row),
        pl.BlockSpec((D_MODEL, D_MODEL), fixed),
    ]
    if router is None:
        return pl.pallas_call(
            _mixer_out_kernel,
            grid=(t // tm,),
            in_specs=in_specs,
            out_specs=pl.BlockSpec((tm, D_MODEL), row),
            out_shape=jax.ShapeDtypeStruct((t, D_MODEL), F32),
            compiler_params=_params(("arbitrary",)),
            name="mixer_out",
        )(attn, conv, x, w_bf16)
    g, wr_hi, wr_lo = router
    return pl.pallas_call(
        _mixer_out_router_kernel,
        grid=(t // tm,),
        in_specs=in_specs + [
            pl.BlockSpec((1, D_MODEL), fixed),
            pl.BlockSpec((D_MODEL, V7X_LANES), fixed),
            pl.BlockSpec((D_MODEL, V7X_LANES), fixed),
        ],
        out_specs=[
            pl.BlockSpec((tm, D_MODEL), row),
            pl.BlockSpec((tm, V7X_LANES), row),
            pl.BlockSpec((V7X_SUBLANES, V7X_LANES), fixed),
        ],
        out_shape=[
            jax.ShapeDtypeStruct((t, D_MODEL), F32),
            jax.ShapeDtypeStruct((t, V7X_LANES), F32),
            jax.ShapeDtypeStruct((V7X_SUBLANES, V7X_LANES), F32),
        ],
        scratch_shapes=[pltpu.VMEM((1, V7X_LANES), F32)],
        compiler_params=_params(("arbitrary",)),
        name="mixer_out_router",
    )(attn, conv, x, w_bf16, g, wr_hi, wr_lo)


def _dense_ffn_kernel(x_ref, g_ref, wg_ref, wu_ref, wd_ref, o_ref, hbf, acc):
    j = pl.program_id(1)

    @pl.when(j == 0)
    def _():
        hbf[...] = _rms_norm_bf16(x_ref[...], g_ref[...])

    h = hbf[...]
    gate = jnp.dot(h, wg_ref[...], preferred_element_type=F32)
    up = jnp.dot(h, wu_ref[...], preferred_element_type=F32)
    a = (gate * _sigmoid(gate) * up).astype(BF16)
    y = jnp.dot(a, wd_ref[...], preferred_element_type=F32)

    @pl.when(j == 0)
    def _():
        acc[...] = y

    @pl.when(j > 0)
    def _():
        acc[...] += y

    @pl.when(j == pl.num_programs(1) - 1)
    def _():
        o_ref[...] = x_ref[...] + acc[...]


def _dense_ffn(x, g, wg, wu, wd):
    t = x.shape[0]
    tm = TM_FFN
    d_ff = wg.shape[1]
    nj = 2
    tf = d_ff // nj
    return pl.pallas_call(
        _dense_ffn_kernel,
        grid=(t // tm, nj),
        in_specs=[
            pl.BlockSpec((tm, D_MODEL), lambda i, j: (i, 0)),
            pl.BlockSpec((1, D_MODEL), lambda i, j: (0, 0)),
            pl.BlockSpec((D_MODEL, tf), lambda i, j: (0, j)),
            pl.BlockSpec((D_MODEL, tf), lambda i, j: (0, j)),
            pl.BlockSpec((tf, D_MODEL), lambda i, j: (j, 0)),
        ],
        out_specs=pl.BlockSpec((tm, D_MODEL), lambda i, j: (i, 0)),
        out_shape=jax.ShapeDtypeStruct((t, D_MODEL), F32),
        scratch_shapes=[pltpu.VMEM((tm, D_MODEL), BF16), pltpu.VMEM((tm, D_MODEL), F32)],
        compiler_params=_params(("arbitrary", "arbitrary")),
        name="dense_ffn",
    )(x, g, wg, wu, wd)


def _token_rows(ref, row):
    return ref.at[pl.ds(pl.multiple_of(row * ROW_CHUNKS, ROW_CHUNKS), ROW_CHUNKS), :]


def _dispatch_kernel(pos_ref, x_ref, xs_in_ref, xs_ref, buf, sem):
    del xs_in_ref
    tm = x_ref.shape[0]
    for c in range(ROW_CHUNKS):
        buf[pl.ds(c, tm, stride=ROW_CHUNKS), :] = x_ref[:, c * V7X_LANES:(c + 1) * V7X_LANES]

    def issue(r, carry):
        src = _token_rows(buf, r)
        for k in range(2):
            dst = _token_rows(xs_ref, pos_ref[0, 0, 2 * r + k])
            pltpu.make_async_copy(src, dst, sem).start()
        return carry

    lax.fori_loop(0, tm, issue, 0)
    for k in range(2):
        pltpu.make_async_copy(buf, xs_ref.at[pl.ds(0, tm * ROW_CHUNKS), :], sem).wait()


def _dispatch(pos_tiles, x, n_rows):
    t = x.shape[0]
    tm = TM_TOK
    zeros = jnp.zeros((n_rows * ROW_CHUNKS, V7X_LANES), F32)
    return pl.pallas_call(
        _dispatch_kernel,
        grid=(t // tm,),
        in_specs=[
            pl.BlockSpec((1, 1, 2 * tm), lambda i: (i, 0, 0), memory_space=pltpu.SMEM),
            pl.BlockSpec((tm, D_MODEL), lambda i: (i, 0)),
            pl.BlockSpec(memory_space=pl.ANY),
        ],
        out_specs=pl.BlockSpec(memory_space=pl.ANY),
        out_shape=jax.ShapeDtypeStruct(zeros.shape, F32),
        scratch_shapes=[pltpu.VMEM((tm * ROW_CHUNKS, V7X_LANES), F32),
                        pltpu.SemaphoreType.DMA(())],
        input_output_aliases={2: 0},
        compiler_params=_params(("arbitrary",)),
        name="moe_dispatch",
    )(pos_tiles, x, zeros)


def _moe_ffn_kernel(te_ref, tv_ref, xs_ref, g_ref, wg_ref, wu_ref, wd_ref, ys_ref, hbf, acc):
    del te_ref
    i = pl.program_id(0)
    j = pl.program_id(1)
    last = pl.num_programs(1) - 1
    tm = hbf.shape[0]
    valid = tv_ref[i] != 0

    @pl.when(valid & (j == 0))
    def _():
        x = jnp.concatenate(
            [xs_ref[pl.ds(c, tm, stride=ROW_CHUNKS), :] for c in range(ROW_CHUNKS)], axis=1)
        hbf[...] = _rms_norm_bf16(x, g_ref[...])

    @pl.when(valid)
    def _():
        h = hbf[...]
        gate = jnp.dot(h, wg_ref[...], preferred_element_type=F32)
        up = jnp.dot(h, wu_ref[...], preferred_element_type=F32)
        a = (gate * _sigmoid(gate) * up).astype(BF16)
        y = jnp.dot(a, wd_ref[...], preferred_element_type=F32)

        @pl.when(j == 0)
        def _():
            acc[...] = y

        @pl.when(j > 0)
        def _():
            acc[...] += y

    @pl.when(valid & (j == last))
    def _():
        for c in range(ROW_CHUNKS):
            ys_ref[pl.ds(c, tm, stride=ROW_CHUNKS), :] = acc[:, c * V7X_LANES:(c + 1) * V7X_LANES]

    @pl.when(jnp.logical_not(valid) & (j == last))
    def _():
        ys_ref[...] = jnp.zeros_like(ys_ref)


def _moe_ffn(tile_expert, tile_valid, xs, g, wg, wu, wd):
    n_tiles = tile_expert.shape[0]
    tm = TM_MOE
    d_ff = wg.shape[2]
    nj = 2
    tf = d_ff // nj

    def jsel(i, j, tv):
        return jnp.where(tv[i] != 0, j, nj - 1)

    grid_spec = pltpu.PrefetchScalarGridSpec(
        num_scalar_prefetch=2,
        grid=(n_tiles, nj),
        in_specs=[
            pl.BlockSpec((tm * ROW_CHUNKS, V7X_LANES), lambda i, j, te, tv: (i, 0)),
            pl.BlockSpec((1, D_MODEL), lambda i, j, te, tv: (0, 0)),
            pl.BlockSpec((None, D_MODEL, tf), lambda i, j, te, tv: (te[i], 0, jsel(i, j, tv))),
            pl.BlockSpec((None, D_MODEL, tf), lambda i, j, te, tv: (te[i], 0, jsel(i, j, tv))),
            pl.BlockSpec((None, tf, D_MODEL), lambda i, j, te, tv: (te[i], jsel(i, j, tv), 0)),
        ],
        out_specs=pl.BlockSpec((tm * ROW_CHUNKS, V7X_LANES), lambda i, j, te, tv: (i, 0)),
        scratch_shapes=[pltpu.VMEM((tm, D_MODEL), BF16), pltpu.VMEM((tm, D_MODEL), F32)],
    )
    return pl.pallas_call(
        _moe_ffn_kernel,
        grid_spec=grid_spec,
        out_shape=jax.ShapeDtypeStruct(xs.shape, F32),
        compiler_params=_params(("arbitrary", "arbitrary")),
        name="moe_ffn",
    )(tile_expert, tile_valid, xs, g, wg, wu, wd)


def _combine_kernel(pos_ref, x_ref, route_ref, ys_ref, o_ref, y1buf, y2buf, sem):
    tm = x_ref.shape[0]

    def issue(r, carry):
        for k, ybuf in ((0, y1buf), (1, y2buf)):
            src = _token_rows(ys_ref, pos_ref[0, 0, 2 * r + k])
            pltpu.make_async_copy(src, _token_rows(ybuf, r), sem).start()
        return carry

    lax.fori_loop(0, tm, issue, 0)
    for ybuf in (y1buf, y2buf):
        pltpu.make_async_copy(ys_ref.at[pl.ds(0, tm * ROW_CHUNKS), :], ybuf, sem).wait()

    w1 = route_ref[:, 2:3]
    w2 = route_ref[:, 3:4]
    for c in range(ROW_CHUNKS):
        cols = slice(c * V7X_LANES, (c + 1) * V7X_LANES)
        y1 = y1buf[pl.ds(c, tm, stride=ROW_CHUNKS), :]
        y2 = y2buf[pl.ds(c, tm, stride=ROW_CHUNKS), :]
        o_ref[:, cols] = x_ref[:, cols] + (w1 * y1 + w2 * y2)


def _combine(pos_tiles, x, route, ys):
    t = x.shape[0]
    tm = TM_TOK
    return pl.pallas_call(
        _combine_kernel,
        grid=(t // tm,),
        in_specs=[
            pl.BlockSpec((1, 1, 2 * tm), lambda i: (i, 0, 0), memory_space=pltpu.SMEM),
            pl.BlockSpec((tm, D_MODEL), lambda i: (i, 0)),
            pl.BlockSpec((tm, V7X_LANES), lambda i: (i, 0)),
            pl.BlockSpec(memory_space=pl.ANY),
        ],
        out_specs=pl.BlockSpec((tm, D_MODEL), lambda i: (i, 0)),
        out_shape=jax.ShapeDtypeStruct((t, D_MODEL), F32),
        scratch_shapes=[pltpu.VMEM((tm * ROW_CHUNKS, V7X_LANES), F32),
                        pltpu.VMEM((tm * ROW_CHUNKS, V7X_LANES), F32),
                        pltpu.SemaphoreType.DMA(())],
        compiler_params=_params(("arbitrary",)),
        name="moe_combine",
    )(pos_tiles, x, route, ys)


def _moe_layer(x, route, counts, g, wg, wu, wd):
    t = x.shape[0]
    tm = TM_MOE
    n_tiles = (2 * t) // tm + N_EXPERTS
    expert = route[:, 0:2].astype(jnp.int32)
    rank = route[:, 4:6].astype(jnp.int32)
    count = counts[0, :N_EXPERTS].astype(jnp.int32)
    padded = ((count + tm - 1) // tm) * tm
    gend = jnp.cumsum(padded)
    gstart = gend - padded
    ids = jnp.arange(N_EXPERTS, dtype=jnp.int32)
    pos = rank + jnp.sum(jnp.where(expert[..., None] == ids, gstart, 0), axis=-1)
    tile_start = jnp.arange(n_tiles, dtype=jnp.int32) * tm
    tile_valid = (tile_start < gend[-1]).astype(jnp.int32)
    last_expert = jnp.max(jnp.where(padded > 0, ids, 0))
    tile_expert = jnp.minimum(
        jnp.sum((tile_start[:, None] >= gend[None, :]).astype(jnp.int32), axis=1), last_expert)
    pos_tiles = pos.reshape(t // TM_TOK, 1, 2 * TM_TOK)

    xs = _dispatch(pos_tiles, x, n_tiles * tm)
    ys = _moe_ffn(tile_expert, tile_valid, xs, g, wg, wu, wd)
    return _combine(pos_tiles, x, route, ys)


def kernel(x, attn_norm_g, w_in, q_norm_g, k_norm_g, sinks, conv_w, conv_b, conv_ln_g,
           conv_ln_b, w_out, ffn_norm_g, dense_w_gate, dense_w_up, dense_w_down,
           w_router, moe_w_gate, moe_w_up, moe_w_down):
    batch, seq, d = x.shape
    depth = w_in.shape[0]
    t = batch * seq
    assert d == D_MODEL and seq % TC_CONV == 0 and t % TM_PROJ == 0 and t % TM_TOK == 0
    assert w_in.shape[2] == IN_PROJ_WIDTH and conv_w.shape[1] == CONV_KERNEL
    assert dense_w_gate.shape[2] % 256 == 0 and moe_w_gate.shape[3] % 256 == 0

    xt = x.reshape(t, d)
    bias = _attention_bias()
    head_id = jnp.arange(QK_WIDTH) // HEAD_DIM
    seg = (head_id[:, None] == head_id[None, :]).astype(BF16)
    scale = HEAD_DIM ** -0.5

    for layer in range(depth):
        gain = jnp.concatenate([jnp.tile(q_norm_g[layer], N_Q_HEADS) * scale,
                                jnp.tile(k_norm_g[layer], N_KV_HEADS)])[None, :]
        q, kd, vd, glu = _mixer_in(xt, attn_norm_g[layer][None, :], w_in[layer].astype(BF16),
                                   gain, seg)
        attn = _attention(q, kd, vd, sinks[layer], bias, batch, seq)
        w_b = jnp.broadcast_to(conv_w[layer][:, None, :], (CONV_KERNEL, V7X_SUBLANES, CONV_WIDTH))
        conv = _conv(glu, w_b, conv_b[layer][None, :], conv_ln_g[layer][None, :],
                     conv_ln_b[layer][None, :], seq)
        ffn_g = ffn_norm_g[layer][None, :]
        i = layer // 2
        if layer % 2 == 0:
            xm = _mixer_out(attn, conv, xt, w_out[layer].astype(BF16))
            xt = _dense_ffn(xm, ffn_g, dense_w_gate[i].astype(BF16), dense_w_up[i].astype(BF16),
                            dense_w_down[i].astype(BF16))
        else:
            wr = jnp.zeros((D_MODEL, V7X_LANES), F32).at[:, :N_EXPERTS].set(w_router[i])
            wr_hi = wr.astype(BF16)
            wr_lo = (wr - wr_hi.astype(F32)).astype(BF16)
            xm, route, counts = _mixer_out(attn, conv, xt, w_out[layer].astype(BF16),
                                           router=(ffn_g, wr_hi, wr_lo))
            xt = _moe_layer(xm, route, counts, ffn_g, moe_w_gate[i].astype(BF16),
                            moe_w_up[i].astype(BF16), moe_w_down[i].astype(BF16))
    return xt.reshape(batch, seq, d)
```

```python
import functools

import jax
import jax.numpy as jnp
from jax import lax
from jax.experimental import pallas as pl
from jax.experimental.pallas import tpu as pltpu

F32 = jnp.float32
BF16 = jnp.bfloat16

D_MODEL = 1024
HEAD_DIM = 64
N_Q_HEADS = 8
N_KV_HEADS = 2
ATTN_WIDTH = N_Q_HEADS * HEAD_DIM
KV_WIDTH = N_KV_HEADS * HEAD_DIM
QK_WIDTH = ATTN_WIDTH + KV_WIDTH
BLOCK = 128
CONV_WIDTH = D_MODEL - ATTN_WIDTH
CONV_KERNEL = 31
IN_PROJ_WIDTH = ATTN_WIDTH + 2 * KV_WIDTH + 2 * CONV_WIDTH
N_EXPERTS = 8
NORM_EPS = 1e-6

V7X_LANES = 128
V7X_SUBLANES = 8
V7X_VMEM_LIMIT_BYTES = 56 * 1024 * 1024
ROW_CHUNKS = D_MODEL // V7X_LANES

TM_PROJ = 512
TC_CONV = 256
CONV_HALO = 32
CONV_SLABS = CONV_WIDTH // V7X_LANES
CONV_SEGS = 4
CONV_SEG_ROWS = TC_CONV // CONV_SEGS
CONV_SEG_LEN = CONV_SEG_ROWS + CONV_HALO
TM_FFN = 512
TM_MOE = 1024
TM_MOE_SUB = 512
TF_MOE = 512
TM_TOK = 512


def _params(sem, vmem=V7X_VMEM_LIMIT_BYTES):
    return pltpu.CompilerParams(dimension_semantics=sem, vmem_limit_bytes=vmem)


def _sigmoid(x):
    return 1.0 / (1.0 + jnp.exp(-x))


def _rms_norm_bf16(x, g):
    ms = jnp.mean(x * x, axis=-1, keepdims=True)
    return (x * lax.rsqrt(ms + NORM_EPS) * g).astype(BF16)


def _mixer_in_kernel(x_ref, g_ref, w_ref, gain_ref, seg_ref, q_ref, k_ref, v_ref, glu_ref):
    h = _rms_norm_bf16(x_ref[...], g_ref[...])
    proj = jnp.dot(h, w_ref[...], preferred_element_type=F32)
    qk = proj[:, :QK_WIDTH]
    sq = qk * qk
    hi = sq.astype(BF16)
    lo = (sq - hi.astype(F32)).astype(BF16)
    seg = seg_ref[...]
    ss = (jnp.dot(hi, seg, preferred_element_type=F32)
          + jnp.dot(lo, seg, preferred_element_type=F32))
    qkn = qk * lax.rsqrt(ss * (1.0 / HEAD_DIM) + NORM_EPS) * gain_ref[...]
    q_ref[...] = qkn[:, :ATTN_WIDTH].astype(BF16)
    lane = lax.broadcasted_iota(jnp.int32, (x_ref.shape[0], KV_WIDTH), 1)
    low = lane < HEAD_DIM
    for src, dst in ((qkn[:, ATTN_WIDTH:QK_WIDTH], k_ref),
                     (proj[:, QK_WIDTH:QK_WIDTH + KV_WIDTH], v_ref)):
        rot = pltpu.roll(src, HEAD_DIM, axis=1)
        dst[:, :KV_WIDTH] = jnp.where(low, src, rot).astype(BF16)
        dst[:, KV_WIDTH:] = jnp.where(low, rot, src).astype(BF16)
    o_c = QK_WIDTH + KV_WIDTH
    u = proj[:, o_c:o_c + CONV_WIDTH]
    gate = proj[:, o_c + CONV_WIDTH:]
    glu_ref[...] = u * _sigmoid(gate)


def _mixer_in(x, g, w_bf16, gain, seg):
    t = x.shape[0]
    tm = TM_PROJ
    row = lambda i: (i, 0)
    fixed = lambda i: (0, 0)
    return pl.pallas_call(
        _mixer_in_kernel,
        grid=(t // tm,),
        in_specs=[
            pl.BlockSpec((tm, D_MODEL), row),
            pl.BlockSpec((1, D_MODEL), fixed),
            pl.BlockSpec((D_MODEL, IN_PROJ_WIDTH), fixed),
            pl.BlockSpec((1, QK_WIDTH), fixed),
            pl.BlockSpec((QK_WIDTH, QK_WIDTH), fixed),
        ],
        out_specs=[
            pl.BlockSpec((tm, ATTN_WIDTH), row),
            pl.BlockSpec((tm, 2 * KV_WIDTH), row),
            pl.BlockSpec((tm, 2 * KV_WIDTH), row),
            pl.BlockSpec((tm, CONV_WIDTH), row),
        ],
        out_shape=[
            jax.ShapeDtypeStruct((t, ATTN_WIDTH), BF16),
            jax.ShapeDtypeStruct((t, 2 * KV_WIDTH), BF16),
            jax.ShapeDtypeStruct((t, 2 * KV_WIDTH), BF16),
            jax.ShapeDtypeStruct((t, CONV_WIDTH), F32),
        ],
        compiler_params=_params(("arbitrary",)),
        name="mixer_in",
    )(x, g, w_bf16, gain, seg)


def _attn_kernel(sink_ref, q_ref, kc_ref, kp_ref, vc_ref, vp_ref, bias_ref, o_ref):
    low = lax.broadcasted_iota(jnp.int32, (2 * BLOCK, 2 * HEAD_DIM), 1) < HEAD_DIM
    low_q = lax.broadcasted_iota(jnp.int32, (BLOCK, 2 * HEAD_DIM), 1) < HEAD_DIM
    zero = jnp.zeros((), BF16)
    contract_last = (((1,), (1,)), ((), ()))
    for g in range(N_KV_HEADS):
        cols = slice(g * 2 * HEAD_DIM, (g + 1) * 2 * HEAD_DIM)
        kd = jnp.concatenate([kp_ref[:, cols], kc_ref[:, cols]], axis=0)
        vd = jnp.concatenate([vp_ref[:, cols], vc_ref[:, cols]], axis=0)
        k_lo = jnp.where(low, kd, zero)
        k_hi = jnp.where(low, zero, kd)
        pair0 = slice((2 * g) * 2 * HEAD_DIM, (2 * g + 1) * 2 * HEAD_DIM)
        pair1 = slice((2 * g + 1) * 2 * HEAD_DIM, (2 * g + 2) * 2 * HEAD_DIM)
        q2 = jnp.concatenate([q_ref[:, pair0], q_ref[:, pair1]], axis=0)
        s_lo = lax.dot_general(q2, k_lo, contract_last, preferred_element_type=F32)
        s_hi = lax.dot_general(q2, k_hi, contract_last, preferred_element_type=F32)
        outs = {}
        for s, first_head in ((s_lo, 4 * g), (s_hi, 4 * g + 1)):
            for half in range(2):
                head = first_head + 2 * half
                sh = s[half * BLOCK:(half + 1) * BLOCK] + bias_ref[0, head]
                sink = sink_ref[head]
                m = jnp.maximum(jnp.max(sh, axis=-1, keepdims=True), sink)
                p = jnp.exp(sh - m)
                den = jnp.sum(p, axis=-1, keepdims=True) + jnp.exp(sink - m)
                pv = jnp.dot(p.astype(BF16), vd, preferred_element_type=F32)
                outs[head] = pv / den
        o_ref[:, pair0] = jnp.where(low_q, outs[4 * g], outs[4 * g + 1]).astype(BF16)
        o_ref[:, pair1] = jnp.where(low_q, outs[4 * g + 2], outs[4 * g + 3]).astype(BF16)


def _attention(q, kd, vd, sinks, bias, batch, seq):
    t = q.shape[0]
    nb = seq // BLOCK
    cur = lambda b, n: (b * nb + n, 0)
    prev = lambda b, n: (b * nb + jnp.maximum(n - 1, 0), 0)
    return pl.pallas_call(
        _attn_kernel,
        grid=(batch, nb),
        in_specs=[
            pl.BlockSpec(memory_space=pltpu.SMEM),
            pl.BlockSpec((BLOCK, ATTN_WIDTH), cur),
            pl.BlockSpec((BLOCK, 2 * KV_WIDTH), cur),
            pl.BlockSpec((BLOCK, 2 * KV_WIDTH), prev),
            pl.BlockSpec((BLOCK, 2 * KV_WIDTH), cur),
            pl.BlockSpec((BLOCK, 2 * KV_WIDTH), prev),
            pl.BlockSpec((1, N_Q_HEADS, BLOCK, 2 * BLOCK),
                         lambda b, n: (jnp.minimum(n, 1), 0, 0, 0)),
        ],
        out_specs=pl.BlockSpec((BLOCK, ATTN_WIDTH), cur),
        out_shape=jax.ShapeDtypeStruct((t, ATTN_WIDTH), BF16),
        compiler_params=_params(("arbitrary", "arbitrary")),
        name="attention",
    )(sinks, q, kd, kd, vd, vd, bias)


def _attention_bias():
    heads = jnp.arange(1, N_Q_HEADS + 1, dtype=F32)
    slopes = jnp.exp2(-8.0 * heads / N_Q_HEADS)
    dist = (jnp.arange(BLOCK)[:, None] + BLOCK) - jnp.arange(2 * BLOCK)[None, :]
    valid = (dist >= 0) & (dist < BLOCK)
    has_prev = jnp.arange(2 * BLOCK)[None, :] >= BLOCK
    valid = jnp.stack([valid & has_prev, valid])
    bias = -slopes[:, None, None] * dist.astype(F32)
    return jnp.where(valid[:, None], bias[None], -jnp.inf)


def _conv_kernel(tiles_per_seq, cur_ref, halo_ref, w_ref, b_ref, lng_ref, lnb_ref, o_ref, buf, ybuf):
    first = (pl.program_id(0) % tiles_per_seq) == 0
    groups = CONV_SEG_LEN // V7X_SUBLANES
    for s in range(CONV_SLABS):
        cols = slice(s * V7X_LANES, (s + 1) * V7X_LANES)
        for j in range(CONV_SEGS):
            for gi in range(groups):
                t0 = CONV_SEG_ROWS * j - CONV_HALO + V7X_SUBLANES * gi
                if t0 < 0:
                    h0 = CONV_HALO + t0
                    src = jnp.where(first, 0.0, halo_ref[h0:h0 + V7X_SUBLANES, cols])
                else:
                    src = cur_ref[t0:t0 + V7X_SUBLANES, cols]
                buf[s, pl.ds(CONV_SEGS * V7X_SUBLANES * gi + j, V7X_SUBLANES, stride=CONV_SEGS), :] = src

    base = CONV_HALO - (CONV_KERNEL - 1)
    out_groups = CONV_SEG_ROWS // V7X_SUBLANES

    def seg_body(it, carry):
        s = it // CONV_SEGS
        j = it % CONV_SEGS
        acc = [None] * out_groups
        for k in range(CONV_KERNEL):
            wk = w_ref[s, k]
            for gi in range(out_groups):
                i = V7X_SUBLANES * gi + base + k
                term = buf[s, pl.ds(CONV_SEGS * i + j, V7X_SUBLANES, stride=CONV_SEGS), :] * wk
                acc[gi] = term if acc[gi] is None else acc[gi] + term
        row0 = pl.multiple_of(j * CONV_SEG_ROWS, CONV_SEG_ROWS)
        ybuf[s, pl.ds(row0, CONV_SEG_ROWS), :] = jnp.concatenate(acc, axis=0)
        return carry

    lax.fori_loop(0, CONV_SLABS * CONV_SEGS, seg_body, 0)

    y = jnp.concatenate([ybuf[s] for s in range(CONV_SLABS)], axis=1) + b_ref[...]
    mu = jnp.mean(y, axis=-1, keepdims=True)
    yc = y - mu
    var = jnp.mean(yc * yc, axis=-1, keepdims=True)
    z = yc * lax.rsqrt(var + NORM_EPS) * lng_ref[...] + lnb_ref[...]
    o_ref[...] = (z * _sigmoid(z)).astype(BF16)


def _conv_weights(w):
    ws = w.reshape(CONV_KERNEL, CONV_SLABS, V7X_LANES).transpose(1, 0, 2)
    return jnp.broadcast_to(ws[:, :, None, :], (CONV_SLABS, CONV_KERNEL, V7X_SUBLANES, V7X_LANES))


def _conv(glu, w_slab, b, lng, lnb, seq):
    t = glu.shape[0]
    tc = TC_CONV
    ratio = tc // CONV_HALO
    fixed = lambda i: (0, 0)
    return pl.pallas_call(
        functools.partial(_conv_kernel, seq // tc),
        grid=(t // tc,),
        in_specs=[
            pl.BlockSpec((tc, CONV_WIDTH), lambda i: (i, 0)),
            pl.BlockSpec((CONV_HALO, CONV_WIDTH), lambda i: (jnp.maximum(i * ratio - 1, 0), 0)),
            pl.BlockSpec((CONV_SLABS, CONV_KERNEL, V7X_SUBLANES, V7X_LANES), lambda i: (0, 0, 0, 0)),
            pl.BlockSpec((1, CONV_WIDTH), fixed),
            pl.BlockSpec((1, CONV_WIDTH), fixed),
            pl.BlockSpec((1, CONV_WIDTH), fixed),
        ],
        out_specs=pl.BlockSpec((tc, CONV_WIDTH), lambda i: (i, 0)),
        out_shape=jax.ShapeDtypeStruct((t, CONV_WIDTH), BF16),
        scratch_shapes=[pltpu.VMEM((CONV_SLABS, CONV_SEGS * CONV_SEG_LEN, V7X_LANES), F32),
                        pltpu.VMEM((CONV_SLABS, tc, V7X_LANES), F32)],
        compiler_params=_params(("arbitrary",)),
        name="conv",
    )(glu, glu, w_slab, b, lng, lnb)


def _mixer_out_kernel(attn_ref, conv_ref, x_ref, w_ref, o_ref):
    w = w_ref[...]
    y = (jnp.dot(attn_ref[...], w[:ATTN_WIDTH], preferred_element_type=F32)
         + jnp.dot(conv_ref[...], w[ATTN_WIDTH:], preferred_element_type=F32))
    o_ref[...] = x_ref[...] + y


def _mixer_out_router_kernel(attn_ref, conv_ref, x_ref, w_ref, g_ref, wr_hi_ref, wr_lo_ref,
                             o_ref, route_ref, count_ref, carry):
    i = pl.program_id(0)
    tm = x_ref.shape[0]

    @pl.when(i == 0)
    def _():
        carry[...] = jnp.zeros_like(carry)

    w = w_ref[...]
    y = (jnp.dot(attn_ref[...], w[:ATTN_WIDTH], preferred_element_type=F32)
         + jnp.dot(conv_ref[...], w[ATTN_WIDTH:], preferred_element_type=F32))
    x = x_ref[...] + y
    o_ref[...] = x

    ms = jnp.mean(x * x, axis=-1, keepdims=True)
    h = x * lax.rsqrt(ms + NORM_EPS) * g_ref[...]
    h_hi = h.astype(BF16)
    h_lo = (h - h_hi.astype(F32)).astype(BF16)
    wr_hi = wr_hi_ref[...]
    logits = (jnp.dot(h_hi, wr_hi, preferred_element_type=F32)
              + jnp.dot(h_lo, wr_hi, preferred_element_type=F32)
              + jnp.dot(h_hi, wr_lo_ref[...], preferred_element_type=F32))

    lane = lax.broadcasted_iota(jnp.int32, (tm, V7X_LANES), 1)
    lane_f = lane.astype(F32)
    neg_inf = -jnp.inf
    lg = jnp.where(lane < N_EXPERTS, logits, neg_inf)
    m1 = jnp.max(lg, axis=-1, keepdims=True)
    i1 = jnp.min(jnp.where(lg == m1, lane_f, float(V7X_LANES)), axis=-1, keepdims=True)
    oh1 = lane_f == i1
    lg2 = jnp.where(oh1, neg_inf, lg)
    m2 = jnp.max(lg2, axis=-1, keepdims=True)
    i2 = jnp.min(jnp.where(lg2 == m2, lane_f, float(V7X_LANES)), axis=-1, keepdims=True)
    oh2 = lane_f == i2
    e = jnp.exp(m2 - m1)
    w1 = 1.0 / (1.0 + e)
    w2 = e / (1.0 + e)

    member = jnp.where(oh1 | oh2, 1.0, 0.0)
    r_id = lax.broadcasted_iota(jnp.int32, (tm, tm), 0)
    c_id = lax.broadcasted_iota(jnp.int32, (tm, tm), 1)
    tri = jnp.where(r_id > c_id, 1.0, 0.0).astype(BF16)
    before = jnp.dot(tri, member.astype(BF16), preferred_element_type=F32) + carry[...]
    rank1 = jnp.sum(jnp.where(oh1, before, 0.0), axis=-1, keepdims=True)
    rank2 = jnp.sum(jnp.where(oh2, before, 0.0), axis=-1, keepdims=True)
    carry[...] = carry[...] + jnp.sum(member, axis=0, keepdims=True)

    route = jnp.where(lane == 0, i1, 0.0)
    route = jnp.where(lane == 1, i2, route)
    route = jnp.where(lane == 2, w1, route)
    route = jnp.where(lane == 3, w2, route)
    route = jnp.where(lane == 4, rank1, route)
    route = jnp.where(lane == 5, rank2, route)
    route_ref[...] = route
    count_ref[...] = jnp.broadcast_to(carry[...], count_ref.shape)


def _mixer_out(attn, conv, x, w_bf16, router=None):
    t = x.shape[0]
    tm = TM_PROJ
    row = lambda i: (i, 0)
    fixed = lambda i: (0, 0)
    in_specs = [
        pl.BlockSpec((tm, ATTN_WIDTH), row),
        pl.BlockSpec((tm, CONV_WIDTH), row),
        pl.BlockSpec((tm, D_MODEL), row),
        pl.BlockSpec((D_MODEL, D_MODEL), fixed),
    ]
    if router is None:
        return pl.pallas_call(
            _mixer_out_kernel,
            grid=(t // tm,),
            in_specs=in_specs,
            out_specs=pl.BlockSpec((tm, D_MODEL), row),
            out_shape=jax.ShapeDtypeStruct((t, D_MODEL), F32),
            compiler_params=_params(("arbitrary",)),
            name="mixer_out",
        )(attn, conv, x, w_bf16)
    g, wr_hi, wr_lo = router
    return pl.pallas_call(
        _mixer_out_router_kernel,
        grid=(t // tm,),
        in_specs=in_specs + [
            pl.BlockSpec((1, D_MODEL), fixed),
            pl.BlockSpec((D_MODEL, V7X_LANES), fixed),
            pl.BlockSpec((D_MODEL, V7X_LANES), fixed),
        ],
        out_specs=[
            pl.BlockSpec((tm, D_MODEL), row),
            pl.BlockSpec((tm, V7X_LANES), row),
            pl.BlockSpec((V7X_SUBLANES, V7X_LANES), fixed),
        ],
        out_shape=[
            jax.ShapeDtypeStruct((t, D_MODEL), F32),
            jax.ShapeDtypeStruct((t, V7X_LANES), F32),
            jax.ShapeDtypeStruct((V7X_SUBLANES, V7X_LANES), F32),
        ],
        scratch_shapes=[pltpu.VMEM((1, V7X_LANES), F32)],
        compiler_params=_params(("arbitrary",)),
        name="mixer_out_router",
    )(attn, conv, x, w_bf16, g, wr_hi, wr_lo)


def _dense_ffn_kernel(x_ref, g_ref, wg_ref, wu_ref, wd_ref, o_ref, hbf, acc):
    j = pl.program_id(1)

    @pl.when(j == 0)
    def _():
        hbf[...] = _rms_norm_bf16(x_ref[...], g_ref[...])
        acc[...] = jnp.zeros_like(acc)

    h = hbf[...]
    gate = jnp.dot(h, wg_ref[...], preferred_element_type=F32)
    up = jnp.dot(h, wu_ref[...], preferred_element_type=F32)
    a = (gate * _sigmoid(gate) * up).astype(BF16)
    acc[...] += jnp.dot(a, wd_ref[...], preferred_element_type=F32)

    @pl.when(j == pl.num_programs(1) - 1)
    def _():
        o_ref[...] = x_ref[...] + acc[...]


def _dense_ffn(x, g, wg, wu, wd):
    t = x.shape[0]
    tm = TM_FFN
    d_ff = wg.shape[1]
    nj = 2
    tf = d_ff // nj
    return pl.pallas_call(
        _dense_ffn_kernel,
        grid=(t // tm, nj),
        in_specs=[
            pl.BlockSpec((tm, D_MODEL), lambda i, j: (i, 0)),
            pl.BlockSpec((1, D_MODEL), lambda i, j: (0, 0)),
            pl.BlockSpec((D_MODEL, tf), lambda i, j: (0, j)),
            pl.BlockSpec((D_MODEL, tf), lambda i, j: (0, j)),
            pl.BlockSpec((tf, D_MODEL), lambda i, j: (j, 0)),
        ],
        out_specs=pl.BlockSpec((tm, D_MODEL), lambda i, j: (i, 0)),
        out_shape=jax.ShapeDtypeStruct((t, D_MODEL), F32),
        scratch_shapes=[pltpu.VMEM((tm, D_MODEL), BF16), pltpu.VMEM((tm, D_MODEL), F32)],
        compiler_params=_params(("arbitrary", "arbitrary")),
        name="dense_ffn",
    )(x, g, wg, wu, wd)


def _token_rows(ref, row):
    return ref.at[pl.ds(pl.multiple_of(row * ROW_CHUNKS, ROW_CHUNKS), ROW_CHUNKS), :]


def _dispatch_kernel(pos_ref, x_ref, xs_in_ref, xs_ref, buf, sem):
    del xs_in_ref
    tm = x_ref.shape[0]
    for c in range(ROW_CHUNKS):
        buf[pl.ds(c, tm, stride=ROW_CHUNKS), :] = x_ref[:, c * V7X_LANES:(c + 1) * V7X_LANES]

    def issue(r, carry):
        src = _token_rows(buf, r)
        for k in range(2):
            dst = _token_rows(xs_ref, pos_ref[0, 0, 2 * r + k])
            pltpu.make_async_copy(src, dst, sem).start(priority=k)
        return carry

    lax.fori_loop(0, tm, issue, 0)
    for k in range(2):
        pltpu.make_async_copy(buf, xs_ref.at[pl.ds(0, tm * ROW_CHUNKS), :], sem).wait()


def _dispatch(pos_tiles, x, n_rows):
    t = x.shape[0]
    tm = TM_TOK
    zeros = jnp.zeros((n_rows * ROW_CHUNKS, V7X_LANES), F32)
    return pl.pallas_call(
        _dispatch_kernel,
        grid=(t // tm,),
        in_specs=[
            pl.BlockSpec((1, 1, 2 * tm), lambda i: (i, 0, 0), memory_space=pltpu.SMEM),
            pl.BlockSpec((tm, D_MODEL), lambda i: (i, 0)),
            pl.BlockSpec(memory_space=pl.ANY),
        ],
        out_specs=pl.BlockSpec(memory_space=pl.ANY),
        out_shape=jax.ShapeDtypeStruct(zeros.shape, F32),
        scratch_shapes=[pltpu.VMEM((tm * ROW_CHUNKS, V7X_LANES), F32),
                        pltpu.SemaphoreType.DMA(())],
        input_output_aliases={2: 0},
        compiler_params=_params(("arbitrary",)),
        name="moe_dispatch",
    )(pos_tiles, x, zeros)


def _moe_ffn_kernel(te_ref, tr_ref, xs_ref, g_ref, wg_ref, wu_ref, wd_ref, ys_ref, hbf, acc):
    del te_ref
    i = pl.program_id(0)
    j = pl.program_id(1)
    last = pl.num_programs(1) - 1
    tm = hbf.shape[0]
    rows = tr_ref[i]

    @pl.when((rows > 0) & (j == 0))
    def _():
        x = jnp.concatenate(
            [xs_ref[pl.ds(c, tm, stride=ROW_CHUNKS), :] for c in range(ROW_CHUNKS)], axis=1)
        hbf[...] = _rms_norm_bf16(x, g_ref[...])
        acc[...] = jnp.zeros_like(acc)

    @pl.when(rows > 0)
    def _():
        wg = wg_ref[...].astype(BF16)
        wu = wu_ref[...].astype(BF16)
        wd = wd_ref[...].astype(BF16)
        for part in range(tm // TM_MOE_SUB):
            part_rows = slice(part * TM_MOE_SUB, (part + 1) * TM_MOE_SUB)

            @pl.when(rows > part * TM_MOE_SUB)
            def _():
                h = hbf[part_rows]
                gate = jnp.dot(h, wg, preferred_element_type=F32)
                up = jnp.dot(h, wu, preferred_element_type=F32)
                a = (gate * _sigmoid(gate) * up).astype(BF16)
                acc[part_rows] += jnp.dot(a, wd, preferred_element_type=F32)

    @pl.when((rows > 0) & (j == last))
    def _():
        for c in range(ROW_CHUNKS):
            ys_ref[pl.ds(c, tm, stride=ROW_CHUNKS), :] = acc[:, c * V7X_LANES:(c + 1) * V7X_LANES]

    @pl.when((rows == 0) & (j == last))
    def _():
        ys_ref[...] = jnp.zeros_like(ys_ref)


def _moe_ffn(tile_expert, tile_rows, xs, g, wg, wu, wd):
    n_tiles = tile_expert.shape[0]
    tm = TM_MOE
    d_ff = wg.shape[2]
    tf = TF_MOE
    nj = d_ff // tf

    def jsel(i, j, tr):
        return jnp.where(tr[i] > 0, j, nj - 1)

    grid_spec = pltpu.PrefetchScalarGridSpec(
        num_scalar_prefetch=2,
        grid=(n_tiles, nj),
        in_specs=[
            pl.BlockSpec((tm * ROW_CHUNKS, V7X_LANES), lambda i, j, te, tr: (i, 0)),
            pl.BlockSpec((1, D_MODEL), lambda i, j, te, tr: (0, 0)),
            pl.BlockSpec((None, D_MODEL, tf), lambda i, j, te, tr: (te[i], 0, jsel(i, j, tr))),
            pl.BlockSpec((None, D_MODEL, tf), lambda i, j, te, tr: (te[i], 0, jsel(i, j, tr))),
            pl.BlockSpec((None, tf, D_MODEL), lambda i, j, te, tr: (te[i], jsel(i, j, tr), 0)),
        ],
        out_specs=pl.BlockSpec((tm * ROW_CHUNKS, V7X_LANES), lambda i, j, te, tr: (i, 0)),
        scratch_shapes=[pltpu.VMEM((tm, D_MODEL), BF16), pltpu.VMEM((tm, D_MODEL), F32)],
    )
    return pl.pallas_call(
        _moe_ffn_kernel,
        grid_spec=grid_spec,
        out_shape=jax.ShapeDtypeStruct(xs.shape, F32),
        compiler_params=_params(("arbitrary", "arbitrary")),
        name="moe_ffn",
    )(tile_expert, tile_rows, xs, g, wg, wu, wd)


def _combine_kernel(pos_ref, x_ref, route_ref, ys_ref, o_ref, y1buf, y2buf, sem):
    tm = x_ref.shape[0]

    def issue(r, carry):
        for k, ybuf in ((0, y1buf), (1, y2buf)):
            src = _token_rows(ys_ref, pos_ref[0, 0, 2 * r + k])
            pltpu.make_async_copy(src, _token_rows(ybuf, r), sem).start(priority=k)
        return carry

    lax.fori_loop(0, tm, issue, 0)
    for ybuf in (y1buf, y2buf):
        pltpu.make_async_copy(ys_ref.at[pl.ds(0, tm * ROW_CHUNKS), :], ybuf, sem).wait()

    w1 = route_ref[:, 2:3]
    w2 = route_ref[:, 3:4]
    for c in range(ROW_CHUNKS):
        cols = slice(c * V7X_LANES, (c + 1) * V7X_LANES)
        y1 = y1buf[pl.ds(c, tm, stride=ROW_CHUNKS), :]
        y2 = y2buf[pl.ds(c, tm, stride=ROW_CHUNKS), :]
        o_ref[:, cols] = x_ref[:, cols] + (w1 * y1 + w2 * y2)


def _combine(pos_tiles, x, route, ys):
    t = x.shape[0]
    tm = TM_TOK
    return pl.pallas_call(
        _combine_kernel,
        grid=(t // tm,),
        in_specs=[
            pl.BlockSpec((1, 1, 2 * tm), lambda i: (i, 0, 0), memory_space=pltpu.SMEM),
            pl.BlockSpec((tm, D_MODEL), lambda i: (i, 0)),
            pl.BlockSpec((tm, V7X_LANES), lambda i: (i, 0)),
            pl.BlockSpec(memory_space=pl.ANY),
        ],
        out_specs=pl.BlockSpec((tm, D_MODEL), lambda i: (i, 0)),
        out_shape=jax.ShapeDtypeStruct((t, D_MODEL), F32),
        scratch_shapes=[pltpu.VMEM((tm * ROW_CHUNKS, V7X_LANES), F32),
                        pltpu.VMEM((tm * ROW_CHUNKS, V7X_LANES), F32),
                        pltpu.SemaphoreType.DMA(())],
        compiler_params=_params(("arbitrary",)),
        name="moe_combine",
    )(pos_tiles, x, route, ys)


def _moe_layer(x, route, counts, g, wg, wu, wd):
    t = x.shape[0]
    tm = TM_MOE
    n_tiles = (2 * t) // tm + N_EXPERTS
    expert = route[:, 0:2].astype(jnp.int32)
    rank = route[:, 4:6].astype(jnp.int32)
    count = counts[0, :N_EXPERTS].astype(jnp.int32)
    padded = ((count + tm - 1) // tm) * tm
    gend = jnp.cumsum(padded)
    gstart = gend - padded
    ids = jnp.arange(N_EXPERTS, dtype=jnp.int32)
    pos = rank + jnp.sum(jnp.where(expert[..., None] == ids, gstart, 0), axis=-1)
    tile_start = jnp.arange(n_tiles, dtype=jnp.int32) * tm
    last_expert = jnp.max(jnp.where(padded > 0, ids, 0))
    tile_expert = jnp.minimum(
        jnp.sum((tile_start[:, None] >= gend[None, :]).astype(jnp.int32), axis=1), last_expert)
    used_end = (gstart + count)[tile_expert]
    tile_rows = jnp.clip(used_end - tile_start, 0, tm)
    pos_tiles = pos.reshape(t // TM_TOK, 1, 2 * TM_TOK)

    xs = _dispatch(pos_tiles, x, n_tiles * tm)
    ys = _moe_ffn(tile_expert, tile_rows, xs, g, wg, wu, wd)
    return _combine(pos_tiles, x, route, ys)


def kernel(x, attn_norm_g, w_in, q_norm_g, k_norm_g, sinks, conv_w, conv_b, conv_ln_g,
           conv_ln_b, w_out, ffn_norm_g, dense_w_gate, dense_w_up, dense_w_down,
           w_router, moe_w_gate, moe_w_up, moe_w_down):
    batch, seq, d = x.shape
    depth = w_in.shape[0]
    t = batch * seq
    assert d == D_MODEL and seq % TC_CONV == 0 and t % TM_PROJ == 0 and t % TM_TOK == 0
    assert w_in.shape[2] == IN_PROJ_WIDTH and conv_w.shape[1] == CONV_KERNEL
    assert dense_w_gate.shape[2] % 256 == 0 and moe_w_gate.shape[3] % TF_MOE == 0

    xt = x.reshape(t, d)
    bias = _attention_bias()
    head_id = jnp.arange(QK_WIDTH) // HEAD_DIM
    seg = (head_id[:, None] == head_id[None, :]).astype(BF16)
    scale = HEAD_DIM ** -0.5

    for layer in range(depth):
        gain = jnp.concatenate([jnp.tile(q_norm_g[layer], N_Q_HEADS) * scale,
                                jnp.tile(k_norm_g[layer], N_KV_HEADS)])[None, :]
        q, kd, vd, glu = _mixer_in(xt, attn_norm_g[layer][None, :], w_in[layer].astype(BF16),
                                   gain, seg)
        attn = _attention(q, kd, vd, sinks[layer], bias, batch, seq)
        conv = _conv(glu, _conv_weights(conv_w[layer]), conv_b[layer][None, :],
                     conv_ln_g[layer][None, :],
                     conv_ln_b[layer][None, :], seq)
        ffn_g = ffn_norm_g[layer][None, :]
        i = layer // 2
        if layer % 2 == 0:
            xm = _mixer_out(attn, conv, xt, w_out[layer].astype(BF16))
            xt = _dense_ffn(xm, ffn_g, dense_w_gate[i].astype(BF16), dense_w_up[i].astype(BF16),
                            dense_w_down[i].astype(BF16))
        else:
            wr = jnp.zeros((D_MODEL, V7X_LANES), F32).at[:, :N_EXPERTS].set(w_router[i])
            wr_hi = wr.astype(BF16)
            wr_lo = (wr - wr_hi.astype(F32)).astype(BF16)
            xm, route, counts = _mixer_out(attn, conv, xt, w_out[layer].astype(BF16),
                                           router=(ffn_g, wr_hi, wr_lo))
            xt = _moe_layer(xm, route, counts, ffn_g, moe_w_gate[i], moe_w_up[i], moe_w_down[i])
    return xt.reshape(batch, seq, d)
```

```python
import functools

import jax
import jax.numpy as jnp
from jax import lax
from jax.experimental import pallas as pl
from jax.experimental.pallas import tpu as pltpu

F32 = jnp.float32
BF16 = jnp.bfloat16

D_MODEL = 1024
HEAD_DIM = 64
N_Q_HEADS = 8
N_KV_HEADS = 2
ATTN_WIDTH = N_Q_HEADS * HEAD_DIM
KV_WIDTH = N_KV_HEADS * HEAD_DIM
QK_WIDTH = ATTN_WIDTH + KV_WIDTH
BLOCK = 128
CONV_WIDTH = D_MODEL - ATTN_WIDTH
CONV_KERNEL = 31
IN_PROJ_WIDTH = ATTN_WIDTH + 2 * KV_WIDTH + 2 * CONV_WIDTH
N_EXPERTS = 8
NORM_EPS = 1e-6

V7X_LANES = 128
V7X_SUBLANES = 8
V7X_VMEM_LIMIT_BYTES = 56 * 1024 * 1024
ROW_CHUNKS = D_MODEL // V7X_LANES

TM_PROJ = 512
TC_CONV = 256
CONV_HALO = 32
CONV_SLABS = CONV_WIDTH // V7X_LANES
CONV_SEGS = 4
CONV_SEG_ROWS = TC_CONV // CONV_SEGS
CONV_SEG_LEN = CONV_SEG_ROWS + CONV_HALO
TM_FFN = 512
TF_DENSE = 256
TM_MOE = 512
TF_MOE = 512
TM_TOK = 512


def _params(sem, vmem=V7X_VMEM_LIMIT_BYTES):
    return pltpu.CompilerParams(dimension_semantics=sem, vmem_limit_bytes=vmem)


def _sigmoid(x):
    return 1.0 / (1.0 + jnp.exp(-x))


def _rms_norm_bf16(x, g):
    ms = jnp.mean(x * x, axis=-1, keepdims=True)
    return (x * lax.rsqrt(ms + NORM_EPS) * g).astype(BF16)


def _mixer_in_kernel(x_ref, g_ref, w_ref, gain_ref, seg_ref, q_ref, k_ref, v_ref, glu_ref, w_bf16):
    @pl.when(pl.program_id(0) == 0)
    def _():
        w_bf16[...] = w_ref[...].astype(BF16)

    h = _rms_norm_bf16(x_ref[...], g_ref[...])
    proj = jnp.dot(h, w_bf16[...], preferred_element_type=F32)
    qk = proj[:, :QK_WIDTH]
    sq = qk * qk
    hi = sq.astype(BF16)
    lo = (sq - hi.astype(F32)).astype(BF16)
    seg = seg_ref[...]
    ss = (jnp.dot(hi, seg, preferred_element_type=F32)
          + jnp.dot(lo, seg, preferred_element_type=F32))
    qkn = qk * lax.rsqrt(ss * (1.0 / HEAD_DIM) + NORM_EPS) * gain_ref[...]
    q_ref[...] = qkn[:, :ATTN_WIDTH].astype(BF16)
    lane = lax.broadcasted_iota(jnp.int32, (x_ref.shape[0], KV_WIDTH), 1)
    low = lane < HEAD_DIM
    for src, dst in ((qkn[:, ATTN_WIDTH:QK_WIDTH], k_ref),
                     (proj[:, QK_WIDTH:QK_WIDTH + KV_WIDTH], v_ref)):
        rot = pltpu.roll(src, HEAD_DIM, axis=1)
        dst[:, :KV_WIDTH] = jnp.where(low, src, rot).astype(BF16)
        dst[:, KV_WIDTH:] = jnp.where(low, rot, src).astype(BF16)
    o_c = QK_WIDTH + KV_WIDTH
    u = proj[:, o_c:o_c + CONV_WIDTH]
    gate = proj[:, o_c + CONV_WIDTH:]
    glu_ref[...] = u * _sigmoid(gate)


def _mixer_in(x, g, w_in, layer, gain, seg):
    t = x.shape[0]
    tm = TM_PROJ
    row = lambda i: (i, 0)
    fixed = lambda i: (0, 0)
    return pl.pallas_call(
        _mixer_in_kernel,
        grid=(t // tm,),
        in_specs=[
            pl.BlockSpec((tm, D_MODEL), row),
            pl.BlockSpec((1, D_MODEL), fixed),
            pl.BlockSpec((None, D_MODEL, IN_PROJ_WIDTH), lambda i: (layer, 0, 0)),
            pl.BlockSpec((1, QK_WIDTH), fixed),
            pl.BlockSpec((QK_WIDTH, QK_WIDTH), fixed),
        ],
        out_specs=[
            pl.BlockSpec((tm, ATTN_WIDTH), row),
            pl.BlockSpec((tm, 2 * KV_WIDTH), row),
            pl.BlockSpec((tm, 2 * KV_WIDTH), row),
            pl.BlockSpec((tm, CONV_WIDTH), row),
        ],
        out_shape=[
            jax.ShapeDtypeStruct((t, ATTN_WIDTH), BF16),
            jax.ShapeDtypeStruct((t, 2 * KV_WIDTH), BF16),
            jax.ShapeDtypeStruct((t, 2 * KV_WIDTH), BF16),
            jax.ShapeDtypeStruct((t, CONV_WIDTH), F32),
        ],
        scratch_shapes=[pltpu.VMEM((D_MODEL, IN_PROJ_WIDTH), BF16)],
        compiler_params=_params(("arbitrary",)),
        name="mixer_in",
    )(x, g, w_in, gain, seg)


def _attn_kernel(sink_ref, q_ref, kc_ref, kp_ref, vc_ref, vp_ref, bias_ref, o_ref):
    low = lax.broadcasted_iota(jnp.int32, (2 * BLOCK, 2 * HEAD_DIM), 1) < HEAD_DIM
    low_q = lax.broadcasted_iota(jnp.int32, (BLOCK, 2 * HEAD_DIM), 1) < HEAD_DIM
    zero = jnp.zeros((), BF16)
    contract_last = (((1,), (1,)), ((), ()))
    for g in range(N_KV_HEADS):
        cols = slice(g * 2 * HEAD_DIM, (g + 1) * 2 * HEAD_DIM)
        kd = jnp.concatenate([kp_ref[:, cols], kc_ref[:, cols]], axis=0)
        vd = jnp.concatenate([vp_ref[:, cols], vc_ref[:, cols]], axis=0)
        k_lo = jnp.where(low, kd, zero)
        k_hi = jnp.where(low, zero, kd)
        pair0 = slice((2 * g) * 2 * HEAD_DIM, (2 * g + 1) * 2 * HEAD_DIM)
        pair1 = slice((2 * g + 1) * 2 * HEAD_DIM, (2 * g + 2) * 2 * HEAD_DIM)
        q2 = jnp.concatenate([q_ref[:, pair0], q_ref[:, pair1]], axis=0)
        s_lo = lax.dot_general(q2, k_lo, contract_last, preferred_element_type=F32)
        s_hi = lax.dot_general(q2, k_hi, contract_last, preferred_element_type=F32)
        outs = {}
        for s, first_head in ((s_lo, 4 * g), (s_hi, 4 * g + 1)):
            for half in range(2):
                head = first_head + 2 * half
                sh = s[half * BLOCK:(half + 1) * BLOCK] + bias_ref[0, head]
                sink = sink_ref[head]
                m = jnp.maximum(jnp.max(sh, axis=-1, keepdims=True), sink)
                p = jnp.exp(sh - m)
                den = jnp.sum(p, axis=-1, keepdims=True) + jnp.exp(sink - m)
                pv = jnp.dot(p.astype(BF16), vd, preferred_element_type=F32)
                outs[head] = pv / den
        o_ref[:, pair0] = jnp.where(low_q, outs[4 * g], outs[4 * g + 1]).astype(BF16)
        o_ref[:, pair1] = jnp.where(low_q, outs[4 * g + 2], outs[4 * g + 3]).astype(BF16)


def _attention(q, kd, vd, sinks, bias, batch, seq):
    t = q.shape[0]
    nb = seq // BLOCK
    cur = lambda b, n: (b * nb + n, 0)
    prev = lambda b, n: (b * nb + jnp.maximum(n - 1, 0), 0)
    return pl.pallas_call(
        _attn_kernel,
        grid=(batch, nb),
        in_specs=[
            pl.BlockSpec(memory_space=pltpu.SMEM),
            pl.BlockSpec((BLOCK, ATTN_WIDTH), cur),
            pl.BlockSpec((BLOCK, 2 * KV_WIDTH), cur),
            pl.BlockSpec((BLOCK, 2 * KV_WIDTH), prev),
            pl.BlockSpec((BLOCK, 2 * KV_WIDTH), cur),
            pl.BlockSpec((BLOCK, 2 * KV_WIDTH), prev),
            pl.BlockSpec((1, N_Q_HEADS, BLOCK, 2 * BLOCK),
                         lambda b, n: (jnp.minimum(n, 1), 0, 0, 0)),
        ],
        out_specs=pl.BlockSpec((BLOCK, ATTN_WIDTH), cur),
        out_shape=jax.ShapeDtypeStruct((t, ATTN_WIDTH), BF16),
        compiler_params=_params(("arbitrary", "arbitrary")),
        name="attention",
    )(sinks, q, kd, kd, vd, vd, bias)


def _attention_bias():
    heads = jnp.arange(1, N_Q_HEADS + 1, dtype=F32)
    slopes = jnp.exp2(-8.0 * heads / N_Q_HEADS)
    dist = (jnp.arange(BLOCK)[:, None] + BLOCK) - jnp.arange(2 * BLOCK)[None, :]
    valid = (dist >= 0) & (dist < BLOCK)
    has_prev = jnp.arange(2 * BLOCK)[None, :] >= BLOCK
    valid = jnp.stack([valid & has_prev, valid])
    bias = -slopes[:, None, None] * dist.astype(F32)
    return jnp.where(valid[:, None], bias[None], -jnp.inf)


def _conv_kernel(tiles_per_seq, cur_ref, halo_ref, w_ref, b_ref, lng_ref, lnb_ref, o_ref, buf, ybuf):
    first = (pl.program_id(0) % tiles_per_seq) == 0
    groups = CONV_SEG_LEN // V7X_SUBLANES
    for s in range(CONV_SLABS):
        cols = slice(s * V7X_LANES, (s + 1) * V7X_LANES)
        for j in range(CONV_SEGS):
            for gi in range(groups):
                t0 = CONV_SEG_ROWS * j - CONV_HALO + V7X_SUBLANES * gi
                if t0 < 0:
                    h0 = CONV_HALO + t0
                    src = jnp.where(first, 0.0, halo_ref[h0:h0 + V7X_SUBLANES, cols])
                else:
                    src = cur_ref[t0:t0 + V7X_SUBLANES, cols]
                buf[s, pl.ds(CONV_SEGS * V7X_SUBLANES * gi + j, V7X_SUBLANES, stride=CONV_SEGS), :] = src

    base = CONV_HALO - (CONV_KERNEL - 1)
    out_groups = CONV_SEG_ROWS // V7X_SUBLANES

    def seg_body(it, carry):
        s = it // CONV_SEGS
        j = it % CONV_SEGS
        acc = [None] * out_groups
        for k in range(CONV_KERNEL):
            wk = w_ref[s, k]
            for gi in range(out_groups):
                i = V7X_SUBLANES * gi + base + k
                term = buf[s, pl.ds(CONV_SEGS * i + j, V7X_SUBLANES, stride=CONV_SEGS), :] * wk
                acc[gi] = term if acc[gi] is None else acc[gi] + term
        row0 = pl.multiple_of(j * CONV_SEG_ROWS, CONV_SEG_ROWS)
        ybuf[s, pl.ds(row0, CONV_SEG_ROWS), :] = jnp.concatenate(acc, axis=0)
        return carry

    lax.fori_loop(0, CONV_SLABS * CONV_SEGS, seg_body, 0)

    y = jnp.concatenate([ybuf[s] for s in range(CONV_SLABS)], axis=1) + b_ref[...]
    mu = jnp.mean(y, axis=-1, keepdims=True)
    yc = y - mu
    var = jnp.mean(yc * yc, axis=-1, keepdims=True)
    z = yc * lax.rsqrt(var + NORM_EPS) * lng_ref[...] + lnb_ref[...]
    o_ref[...] = (z * _sigmoid(z)).astype(BF16)


def _conv_weights(w):
    ws = w.reshape(CONV_KERNEL, CONV_SLABS, V7X_LANES).transpose(1, 0, 2)
    return jnp.broadcast_to(ws[:, :, None, :], (CONV_SLABS, CONV_KERNEL, V7X_SUBLANES, V7X_LANES))


def _conv(glu, w_slab, b, lng, lnb, seq):
    t = glu.shape[0]
    tc = TC_CONV
    ratio = tc // CONV_HALO
    fixed = lambda i: (0, 0)
    return pl.pallas_call(
        functools.partial(_conv_kernel, seq // tc),
        grid=(t // tc,),
        in_specs=[
            pl.BlockSpec((tc, CONV_WIDTH), lambda i: (i, 0)),
            pl.BlockSpec((CONV_HALO, CONV_WIDTH), lambda i: (jnp.maximum(i * ratio - 1, 0), 0)),
            pl.BlockSpec((CONV_SLABS, CONV_KERNEL, V7X_SUBLANES, V7X_LANES), lambda i: (0, 0, 0, 0)),
            pl.BlockSpec((1, CONV_WIDTH), fixed),
            pl.BlockSpec((1, CONV_WIDTH), fixed),
            pl.BlockSpec((1, CONV_WIDTH), fixed),
        ],
        out_specs=pl.BlockSpec((tc, CONV_WIDTH), lambda i: (i, 0)),
        out_shape=jax.ShapeDtypeStruct((t, CONV_WIDTH), BF16),
        scratch_shapes=[pltpu.VMEM((CONV_SLABS, CONV_SEGS * CONV_SEG_LEN, V7X_LANES), F32),
                        pltpu.VMEM((CONV_SLABS, tc, V7X_LANES), F32)],
        compiler_params=_params(("arbitrary",)),
        name="conv",
    )(glu, glu, w_slab, b, lng, lnb)


def _out_proj_residual(attn_ref, conv_ref, x_ref, w_ref, w_bf16):
    @pl.when(pl.program_id(0) == 0)
    def _():
        w_bf16[...] = w_ref[...].astype(BF16)

    y = (jnp.dot(attn_ref[...], w_bf16[:ATTN_WIDTH], preferred_element_type=F32)
         + jnp.dot(conv_ref[...], w_bf16[ATTN_WIDTH:], preferred_element_type=F32))
    return x_ref[...] + y


def _mixer_out_kernel(attn_ref, conv_ref, x_ref, w_ref, o_ref, w_bf16):
    o_ref[...] = _out_proj_residual(attn_ref, conv_ref, x_ref, w_ref, w_bf16)


def _mixer_out_router_kernel(attn_ref, conv_ref, x_ref, w_ref, g_ref, wr_hi_ref, wr_lo_ref,
                             o_ref, route_ref, count_ref, w_bf16, carry):
    i = pl.program_id(0)
    tm = x_ref.shape[0]

    @pl.when(i == 0)
    def _():
        carry[...] = jnp.zeros_like(carry)

    x = _out_proj_residual(attn_ref, conv_ref, x_ref, w_ref, w_bf16)
    o_ref[...] = x

    ms = jnp.mean(x * x, axis=-1, keepdims=True)
    h = x * lax.rsqrt(ms + NORM_EPS) * g_ref[...]
    h_hi = h.astype(BF16)
    h_lo = (h - h_hi.astype(F32)).astype(BF16)
    wr_hi = wr_hi_ref[...]
    logits = (jnp.dot(h_hi, wr_hi, preferred_element_type=F32)
              + jnp.dot(h_lo, wr_hi, preferred_element_type=F32)
              + jnp.dot(h_hi, wr_lo_ref[...], preferred_element_type=F32))

    lane = lax.broadcasted_iota(jnp.int32, (tm, V7X_LANES), 1)
    lane_f = lane.astype(F32)
    neg_inf = -jnp.inf
    lg = jnp.where(lane < N_EXPERTS, logits, neg_inf)
    m1 = jnp.max(lg, axis=-1, keepdims=True)
    i1 = jnp.min(jnp.where(lg == m1, lane_f, float(V7X_LANES)), axis=-1, keepdims=True)
    oh1 = lane_f == i1
    lg2 = jnp.where(oh1, neg_inf, lg)
    m2 = jnp.max(lg2, axis=-1, keepdims=True)
    i2 = jnp.min(jnp.where(lg2 == m2, lane_f, float(V7X_LANES)), axis=-1, keepdims=True)
    oh2 = lane_f == i2
    e = jnp.exp(m2 - m1)
    w1 = 1.0 / (1.0 + e)
    w2 = e / (1.0 + e)

    member = jnp.where(oh1 | oh2, 1.0, 0.0)
    r_id = lax.broadcasted_iota(jnp.int32, (tm, tm), 0)
    c_id = lax.broadcasted_iota(jnp.int32, (tm, tm), 1)
    tri = jnp.where(r_id > c_id, 1.0, 0.0).astype(BF16)
    before = jnp.dot(tri, member.astype(BF16), preferred_element_type=F32) + carry[...]
    rank1 = jnp.sum(jnp.where(oh1, before, 0.0), axis=-1, keepdims=True)
    rank2 = jnp.sum(jnp.where(oh2, before, 0.0), axis=-1, keepdims=True)
    carry[...] = carry[...] + jnp.sum(member, axis=0, keepdims=True)

    route = jnp.where(lane == 0, i1, 0.0)
    route = jnp.where(lane == 1, i2, route)
    route = jnp.where(lane == 2, w1, route)
    route = jnp.where(lane == 3, w2, route)
    route = jnp.where(lane == 4, rank1, route)
    route = jnp.where(lane == 5, rank2, route)
    route_ref[...] = route
    count_ref[...] = jnp.broadcast_to(carry[...], count_ref.shape)


def _mixer_out(attn, conv, x, w_out, layer, router=None):
    t = x.shape[0]
    tm = TM_PROJ
    row = lambda i: (i, 0)
    fixed = lambda i: (0, 0)
    in_specs = [
        pl.BlockSpec((tm, ATTN_WIDTH), row),
        pl.BlockSpec((tm, CONV_WIDTH), row),
        pl.BlockSpec((tm, D_MODEL), row),
        pl.BlockSpec((None, D_MODEL, D_MODEL), lambda i: (layer, 0, 0)),
    ]
    w_scratch = pltpu.VMEM((D_MODEL, D_MODEL), BF16)
    if router is None:
        return pl.pallas_call(
            _mixer_out_kernel,
            grid=(t // tm,),
            in_specs=in_specs,
            out_specs=pl.BlockSpec((tm, D_MODEL), row),
            out_shape=jax.ShapeDtypeStruct((t, D_MODEL), F32),
            scratch_shapes=[w_scratch],
            compiler_params=_params(("arbitrary",)),
            name="mixer_out",
        )(attn, conv, x, w_out)
    g, wr_hi, wr_lo = router
    return pl.pallas_call(
        _mixer_out_router_kernel,
        grid=(t // tm,),
        in_specs=in_specs + [
            pl.BlockSpec((1, D_MODEL), fixed),
            pl.BlockSpec((D_MODEL, V7X_LANES), fixed),
            pl.BlockSpec((D_MODEL, V7X_LANES), fixed),
        ],
        out_specs=[
            pl.BlockSpec((tm, D_MODEL), row),
            pl.BlockSpec((tm, V7X_LANES), row),
            pl.BlockSpec((V7X_SUBLANES, V7X_LANES), fixed),
        ],
        out_shape=[
            jax.ShapeDtypeStruct((t, D_MODEL), F32),
            jax.ShapeDtypeStruct((t, V7X_LANES), F32),
            jax.ShapeDtypeStruct((V7X_SUBLANES, V7X_LANES), F32),
        ],
        scratch_shapes=[w_scratch, pltpu.VMEM((1, V7X_LANES), F32)],
        compiler_params=_params(("arbitrary",)),
        name="mixer_out_router",
    )(attn, conv, x, w_out, g, wr_hi, wr_lo)


def _ffn_chunk_copies(w_hbm, lead, j, tf, stage, sem, slot):
    cols = pl.ds(j * tf, tf)
    every = slice(None)
    index = (lead + (every, cols), lead + (every, cols), lead + (cols, every))
    return [pltpu.make_async_copy(w_hbm[m].at[index[m]], stage[m].at[slot], sem.at[m, slot])
            for m in range(3)]


def _swiglu_tile(load_weights, hbf, w_hbm, lead, res, stage, sem, acc, nj, tf):
    @pl.when(load_weights)
    def _():
        for c in _ffn_chunk_copies(w_hbm, lead, 0, tf, stage, sem, 0):
            c.start()

    y = None
    for j in range(nj):
        slot = j % 2

        @pl.when(load_weights)
        def _():
            if j + 1 < nj:
                for c in _ffn_chunk_copies(w_hbm, lead, j + 1, tf, stage, sem, 1 - slot):
                    c.start()
            for c in _ffn_chunk_copies(w_hbm, lead, j, tf, stage, sem, slot):
                c.wait()
            for m in range(3):
                res[m][j] = stage[m][slot].astype(BF16)

        h = hbf[...]
        gate = jnp.dot(h, res[0][j], preferred_element_type=F32)
        up = jnp.dot(h, res[1][j], preferred_element_type=F32)
        a = (gate * _sigmoid(gate) * up).astype(BF16)
        yj = jnp.dot(a, res[2][j], preferred_element_type=F32)
        if nj == 1:
            y = yj
        elif j == 0:
            acc[...] = yj
        elif j < nj - 1:
            acc[...] += yj
        else:
            y = acc[...] + yj
    return y


def _ffn_scratch(tm, nj, tf):
    return [
        pltpu.VMEM((nj, D_MODEL, tf), BF16),
        pltpu.VMEM((nj, D_MODEL, tf), BF16),
        pltpu.VMEM((nj, tf, D_MODEL), BF16),
        pltpu.VMEM((2, D_MODEL, tf), F32),
        pltpu.VMEM((2, D_MODEL, tf), F32),
        pltpu.VMEM((2, tf, D_MODEL), F32),
        pltpu.SemaphoreType.DMA((3, 2)),
        pltpu.VMEM((tm, D_MODEL), BF16),
        pltpu.VMEM((tm, D_MODEL), F32),
    ]


def _dense_ffn_kernel(layer, nj, tf, x_ref, g_ref, wg, wu, wd, o_ref,
                      rg, ru, rd, sg, su, sd, sem, hbf, acc):
    hbf[...] = _rms_norm_bf16(x_ref[...], g_ref[...])
    y = _swiglu_tile(pl.program_id(0) == 0, hbf, (wg, wu, wd), (layer,), (rg, ru, rd),
                     (sg, su, sd), sem, acc, nj, tf)
    o_ref[...] = x_ref[...] + y


def _dense_ffn(x, g, wg, wu, wd, layer):
    t = x.shape[0]
    tm = TM_FFN
    tf = TF_DENSE
    nj = wg.shape[2] // tf
    hbm = pl.BlockSpec(memory_space=pl.ANY)
    return pl.pallas_call(
        functools.partial(_dense_ffn_kernel, layer, nj, tf),
        grid=(t // tm,),
        in_specs=[
            pl.BlockSpec((tm, D_MODEL), lambda i: (i, 0)),
            pl.BlockSpec((1, D_MODEL), lambda i: (0, 0)),
            hbm, hbm, hbm,
        ],
        out_specs=pl.BlockSpec((tm, D_MODEL), lambda i: (i, 0)),
        out_shape=jax.ShapeDtypeStruct((t, D_MODEL), F32),
        scratch_shapes=_ffn_scratch(tm, nj, tf),
        compiler_params=_params(("arbitrary",)),
        name="dense_ffn",
    )(x, g, wg, wu, wd)


def _token_rows(ref, row):
    return ref.at[pl.ds(pl.multiple_of(row * ROW_CHUNKS, ROW_CHUNKS), ROW_CHUNKS), :]


def _dispatch_kernel(pos_ref, x_ref, xs_in_ref, xs_ref, buf, sem):
    del xs_in_ref
    tm = x_ref.shape[0]
    for c in range(ROW_CHUNKS):
        buf[pl.ds(c, tm, stride=ROW_CHUNKS), :] = x_ref[:, c * V7X_LANES:(c + 1) * V7X_LANES]

    def issue(r, carry):
        src = _token_rows(buf, r)
        for k in range(2):
            dst = _token_rows(xs_ref, pos_ref[0, 0, 2 * r + k])
            pltpu.make_async_copy(src, dst, sem).start(priority=k)
        return carry

    lax.fori_loop(0, tm, issue, 0)
    for k in range(2):
        pltpu.make_async_copy(buf, xs_ref.at[pl.ds(0, tm * ROW_CHUNKS), :], sem).wait()


def _dispatch(pos_tiles, x, n_rows):
    t = x.shape[0]
    tm = TM_TOK
    zeros = jnp.zeros((n_rows * ROW_CHUNKS, V7X_LANES), F32)
    return pl.pallas_call(
        _dispatch_kernel,
        grid=(t // tm,),
        in_specs=[
            pl.BlockSpec((1, 1, 2 * tm), lambda i: (i, 0, 0), memory_space=pltpu.SMEM),
            pl.BlockSpec((tm, D_MODEL), lambda i: (i, 0)),
            pl.BlockSpec(memory_space=pl.ANY),
        ],
        out_specs=pl.BlockSpec(memory_space=pl.ANY),
        out_shape=jax.ShapeDtypeStruct(zeros.shape, F32),
        scratch_shapes=[pltpu.VMEM((tm * ROW_CHUNKS, V7X_LANES), F32),
                        pltpu.SemaphoreType.DMA(())],
        input_output_aliases={2: 0},
        compiler_params=_params(("arbitrary",)),
        name="moe_dispatch",
    )(pos_tiles, x, zeros)


def _moe_ffn_kernel(layer, nj, tf, te_ref, tr_ref, tl_ref, xs_ref, g_ref, wg, wu, wd, ys_ref,
                    rg, ru, rd, sg, su, sd, sem, hbf, acc):
    i = pl.program_id(0)
    tm = hbf.shape[0]
    rows = tr_ref[i]

    @pl.when(rows > 0)
    def _():
        x = jnp.concatenate(
            [xs_ref[pl.ds(c, tm, stride=ROW_CHUNKS), :] for c in range(ROW_CHUNKS)], axis=1)
        hbf[...] = _rms_norm_bf16(x, g_ref[...])
        y = _swiglu_tile(tl_ref[i] != 0, hbf, (wg, wu, wd), (layer, te_ref[i]), (rg, ru, rd),
                         (sg, su, sd), sem, acc, nj, tf)
        for c in range(ROW_CHUNKS):
            ys_ref[pl.ds(c, tm, stride=ROW_CHUNKS), :] = y[:, c * V7X_LANES:(c + 1) * V7X_LANES]

    @pl.when(rows == 0)
    def _():
        ys_ref[...] = jnp.zeros_like(ys_ref)


def _moe_ffn(tile_expert, tile_rows, tile_load, xs, g, wg, wu, wd, layer):
    n_tiles = tile_expert.shape[0]
    tm = TM_MOE
    tf = TF_MOE
    nj = wg.shape[3] // tf
    hbm = pl.BlockSpec(memory_space=pl.ANY)
    grid_spec = pltpu.PrefetchScalarGridSpec(
        num_scalar_prefetch=3,
        grid=(n_tiles,),
        in_specs=[
            pl.BlockSpec((tm * ROW_CHUNKS, V7X_LANES), lambda i, te, tr, tl: (i, 0)),
            pl.BlockSpec((1, D_MODEL), lambda i, te, tr, tl: (0, 0)),
            hbm, hbm, hbm,
        ],
        out_specs=pl.BlockSpec((tm * ROW_CHUNKS, V7X_LANES), lambda i, te, tr, tl: (i, 0)),
        scratch_shapes=_ffn_scratch(tm, nj, tf),
    )
    return pl.pallas_call(
        functools.partial(_moe_ffn_kernel, layer, nj, tf),
        grid_spec=grid_spec,
        out_shape=jax.ShapeDtypeStruct(xs.shape, F32),
        compiler_params=_params(("arbitrary",)),
        name="moe_ffn",
    )(tile_expert, tile_rows, tile_load, xs, g, wg, wu, wd)


def _combine_kernel(pos_ref, x_ref, route_ref, ys_ref, o_ref, y1buf, y2buf, sem):
    tm = x_ref.shape[0]

    def issue(r, carry):
        for k, ybuf in ((0, y1buf), (1, y2buf)):
            src = _token_rows(ys_ref, pos_ref[0, 0, 2 * r + k])
            pltpu.make_async_copy(src, _token_rows(ybuf, r), sem).start(priority=k)
        return carry

    lax.fori_loop(0, tm, issue, 0)
    for ybuf in (y1buf, y2buf):
        pltpu.make_async_copy(ys_ref.at[pl.ds(0, tm * ROW_CHUNKS), :], ybuf, sem).wait()

    w1 = route_ref[:, 2:3]
    w2 = route_ref[:, 3:4]
    for c in range(ROW_CHUNKS):
        cols = slice(c * V7X_LANES, (c + 1) * V7X_LANES)
        y1 = y1buf[pl.ds(c, tm, stride=ROW_CHUNKS), :]
        y2 = y2buf[pl.ds(c, tm, stride=ROW_CHUNKS), :]
        o_ref[:, cols] = x_ref[:, cols] + (w1 * y1 + w2 * y2)


def _combine(pos_tiles, x, route, ys):
    t = x.shape[0]
    tm = TM_TOK
    return pl.pallas_call(
        _combine_kernel,
        grid=(t // tm,),
        in_specs=[
            pl.BlockSpec((1, 1, 2 * tm), lambda i: (i, 0, 0), memory_space=pltpu.SMEM),
            pl.BlockSpec((tm, D_MODEL), lambda i: (i, 0)),
            pl.BlockSpec((tm, V7X_LANES), lambda i: (i, 0)),
            pl.BlockSpec(memory_space=pl.ANY),
        ],
        out_specs=pl.BlockSpec((tm, D_MODEL), lambda i: (i, 0)),
        out_shape=jax.ShapeDtypeStruct((t, D_MODEL), F32),
        scratch_shapes=[pltpu.VMEM((tm * ROW_CHUNKS, V7X_LANES), F32),
                        pltpu.VMEM((tm * ROW_CHUNKS, V7X_LANES), F32),
                        pltpu.SemaphoreType.DMA(())],
        compiler_params=_params(("arbitrary",)),
        name="moe_combine",
    )(pos_tiles, x, route, ys)


def _moe_layer(x, route, counts, g, wg, wu, wd, layer):
    t = x.shape[0]
    tm = TM_MOE
    n_tiles = (2 * t) // tm + N_EXPERTS
    expert = route[:, 0:2].astype(jnp.int32)
    rank = route[:, 4:6].astype(jnp.int32)
    count = counts[0, :N_EXPERTS].astype(jnp.int32)
    padded = ((count + tm - 1) // tm) * tm
    gend = jnp.cumsum(padded)
    gstart = gend - padded
    ids = jnp.arange(N_EXPERTS, dtype=jnp.int32)
    pos = rank + jnp.sum(jnp.where(expert[..., None] == ids, gstart, 0), axis=-1)
    tile_start = jnp.arange(n_tiles, dtype=jnp.int32) * tm
    last_expert = jnp.max(jnp.where(padded > 0, ids, 0))
    tile_expert = jnp.minimum(
        jnp.sum((tile_start[:, None] >= gend[None, :]).astype(jnp.int32), axis=1), last_expert)
    used_end = (gstart + count)[tile_expert]
    tile_rows = jnp.clip(used_end - tile_start, 0, tm)
    prev_expert = jnp.concatenate([jnp.full((1,), -1, jnp.int32), tile_expert[:-1]])
    tile_load = ((tile_rows > 0) & (tile_expert != prev_expert)).astype(jnp.int32)
    pos_tiles = pos.reshape(t // TM_TOK, 1, 2 * TM_TOK)

    xs = _dispatch(pos_tiles, x, n_tiles * tm)
    ys = _moe_ffn(tile_expert, tile_rows, tile_load, xs, g, wg, wu, wd, layer)
    return _combine(pos_tiles, x, route, ys)


def kernel(x, attn_norm_g, w_in, q_norm_g, k_norm_g, sinks, conv_w, conv_b, conv_ln_g,
           conv_ln_b, w_out, ffn_norm_g, dense_w_gate, dense_w_up, dense_w_down,
           w_router, moe_w_gate, moe_w_up, moe_w_down):
    batch, seq, d = x.shape
    depth = w_in.shape[0]
    t = batch * seq
    assert d == D_MODEL and seq % TC_CONV == 0 and t % TM_PROJ == 0 and t % TM_TOK == 0
    assert w_in.shape[2] == IN_PROJ_WIDTH and conv_w.shape[1] == CONV_KERNEL
    assert dense_w_gate.shape[2] % TF_DENSE == 0 and moe_w_gate.shape[3] % TF_MOE == 0

    xt = x.reshape(t, d)
    bias = _attention_bias()
    head_id = jnp.arange(QK_WIDTH) // HEAD_DIM
    seg = (head_id[:, None] == head_id[None, :]).astype(BF16)
    scale = HEAD_DIM ** -0.5

    for layer in range(depth):
        gain = jnp.concatenate([jnp.tile(q_norm_g[layer], N_Q_HEADS) * scale,
                                jnp.tile(k_norm_g[layer], N_KV_HEADS)])[None, :]
        q, kd, vd, glu = _mixer_in(xt, attn_norm_g[layer][None, :], w_in, layer, gain, seg)
        attn = _attention(q, kd, vd, sinks[layer], bias, batch, seq)
        conv = _conv(glu, _conv_weights(conv_w[layer]), conv_b[layer][None, :],
                     conv_ln_g[layer][None, :],
                     conv_ln_b[layer][None, :], seq)
        ffn_g = ffn_norm_g[layer][None, :]
        i = layer // 2
        if layer % 2 == 0:
            xm = _mixer_out(attn, conv, xt, w_out, layer)
            xt = _dense_ffn(xm, ffn_g, dense_w_gate, dense_w_up, dense_w_down, i)
        else:
            wr = jnp.zeros((D_MODEL, V7X_LANES), F32).at[:, :N_EXPERTS].set(w_router[i])
            wr_hi = wr.astype(BF16)
            wr_lo = (wr - wr_hi.astype(F32)).astype(BF16)
            xm, route, counts = _mixer_out(attn, conv, xt, w_out, layer,
                                           router=(ffn_g, wr_hi, wr_lo))
            xt = _moe_layer(xm, route, counts, ffn_g, moe_w_gate, moe_w_up, moe_w_down, i)
    return xt.reshape(batch, seq, d)
```

```python
import functools

import jax
import jax.numpy as jnp
from jax import lax
from jax.experimental import pallas as pl
from jax.experimental.pallas import tpu as pltpu

F32 = jnp.float32
BF16 = jnp.bfloat16

D_MODEL = 1024
HEAD_DIM = 64
N_Q_HEADS = 8
N_KV_HEADS = 2
ATTN_WIDTH = N_Q_HEADS * HEAD_DIM
KV_WIDTH = N_KV_HEADS * HEAD_DIM
QK_WIDTH = ATTN_WIDTH + KV_WIDTH
SEG_WIDTH = 256
BLOCK = 128
CONV_WIDTH = D_MODEL - ATTN_WIDTH
CONV_KERNEL = 31
IN_PROJ_WIDTH = ATTN_WIDTH + 2 * KV_WIDTH + 2 * CONV_WIDTH
N_EXPERTS = 8
NORM_EPS = 1e-6

V7X_LANES = 128
V7X_SUBLANES = 8
V7X_VMEM_LIMIT_BYTES = 56 * 1024 * 1024
ROW_CHUNKS = D_MODEL // V7X_LANES

TM_PROJ = 512
ATTN_BLOCKS = 4
TC_CONV = 256
CONV_HALO = 32
CONV_SLABS = CONV_WIDTH // V7X_LANES
CONV_SEGS = 4
CONV_SEG_ROWS = TC_CONV // CONV_SEGS
CONV_SEG_LEN = CONV_SEG_ROWS + CONV_HALO
TM_FFN = 512
TF_DENSE = 256
TM_MOE = 512
TF_MOE = 512
TM_TOK = 512


def _params(sem, vmem=V7X_VMEM_LIMIT_BYTES):
    return pltpu.CompilerParams(dimension_semantics=sem, vmem_limit_bytes=vmem)


def _sigmoid(x):
    return 1.0 / (1.0 + jnp.exp(-x))


def _rms_norm_bf16(x, g):
    ms = jnp.mean(x * x, axis=-1, keepdims=True)
    return (x * lax.rsqrt(ms + NORM_EPS) * g).astype(BF16)


def _mixer_in_kernel(x_ref, g_ref, w_ref, gain_ref, seg_ref, q_ref, k_ref, v_ref, glu_ref, w_bf16):
    @pl.when(pl.program_id(0) == 0)
    def _():
        w_bf16[...] = w_ref[...].astype(BF16)

    h = _rms_norm_bf16(x_ref[...], g_ref[...])
    proj = jnp.dot(h, w_bf16[...], preferred_element_type=F32)
    qk = proj[:, :QK_WIDTH]
    sq = qk * qk
    hi = sq.astype(BF16)
    lo = (sq - hi.astype(F32)).astype(BF16)
    seg = seg_ref[...]
    parts = []
    for c0 in range(0, QK_WIDTH, SEG_WIDTH):
        width = min(SEG_WIDTH, QK_WIDTH - c0)
        blk = slice(c0, c0 + width)
        parts.append(jnp.dot(hi[:, blk], seg[:width, :width], preferred_element_type=F32)
                     + jnp.dot(lo[:, blk], seg[:width, :width], preferred_element_type=F32))
    ss = jnp.concatenate(parts, axis=1)
    qkn = qk * lax.rsqrt(ss * (1.0 / HEAD_DIM) + NORM_EPS) * gain_ref[...]
    q_ref[...] = qkn[:, :ATTN_WIDTH].astype(BF16)
    lane = lax.broadcasted_iota(jnp.int32, (x_ref.shape[0], KV_WIDTH), 1)
    low = lane < HEAD_DIM
    for src, dst in ((qkn[:, ATTN_WIDTH:QK_WIDTH], k_ref),
                     (proj[:, QK_WIDTH:QK_WIDTH + KV_WIDTH], v_ref)):
        rot = pltpu.roll(src, HEAD_DIM, axis=1)
        dst[:, :KV_WIDTH] = jnp.where(low, src, rot).astype(BF16)
        dst[:, KV_WIDTH:] = jnp.where(low, rot, src).astype(BF16)
    o_c = QK_WIDTH + KV_WIDTH
    u = proj[:, o_c:o_c + CONV_WIDTH]
    gate = proj[:, o_c + CONV_WIDTH:]
    glu_ref[...] = u * _sigmoid(gate)


def _mixer_in(x, g, w_in, layer, gain, seg):
    t = x.shape[0]
    tm = TM_PROJ
    row = lambda i: (i, 0)
    fixed = lambda i: (0, 0)
    return pl.pallas_call(
        _mixer_in_kernel,
        grid=(t // tm,),
        in_specs=[
            pl.BlockSpec((tm, D_MODEL), row),
            pl.BlockSpec((1, D_MODEL), fixed),
            pl.BlockSpec((None, D_MODEL, IN_PROJ_WIDTH), lambda i: (layer, 0, 0)),
            pl.BlockSpec((1, QK_WIDTH), fixed),
            pl.BlockSpec((SEG_WIDTH, SEG_WIDTH), fixed),
        ],
        out_specs=[
            pl.BlockSpec((tm, ATTN_WIDTH), row),
            pl.BlockSpec((tm, 2 * KV_WIDTH), row),
            pl.BlockSpec((tm, 2 * KV_WIDTH), row),
            pl.BlockSpec((tm, CONV_WIDTH), row),
        ],
        out_shape=[
            jax.ShapeDtypeStruct((t, ATTN_WIDTH), BF16),
            jax.ShapeDtypeStruct((t, 2 * KV_WIDTH), BF16),
            jax.ShapeDtypeStruct((t, 2 * KV_WIDTH), BF16),
            jax.ShapeDtypeStruct((t, CONV_WIDTH), F32),
        ],
        scratch_shapes=[pltpu.VMEM((D_MODEL, IN_PROJ_WIDTH), BF16)],
        compiler_params=_params(("arbitrary",)),
        name="mixer_in",
    )(x, g, w_in, gain, seg)


def _attn_kernel(sink_ref, q_ref, kc_ref, kp_ref, vc_ref, vp_ref, bias0_ref, bias_ref, o_ref):
    n_keys = (ATTN_BLOCKS + 1) * BLOCK
    low = lax.broadcasted_iota(jnp.int32, (n_keys, 2 * HEAD_DIM), 1) < HEAD_DIM
    low_q = lax.broadcasted_iota(jnp.int32, (BLOCK, 2 * HEAD_DIM), 1) < HEAD_DIM
    zero = jnp.zeros((), BF16)
    contract_last = (((1,), (1,)), ((), ()))
    for g in range(N_KV_HEADS):
        cols = slice(g * 2 * HEAD_DIM, (g + 1) * 2 * HEAD_DIM)
        k_all = jnp.concatenate([kp_ref[:, cols], kc_ref[:, cols]], axis=0)
        v_all = jnp.concatenate([vp_ref[:, cols], vc_ref[:, cols]], axis=0)
        k_lo_all = jnp.where(low, k_all, zero)
        k_hi_all = jnp.where(low, zero, k_all)
        pair0 = slice((2 * g) * 2 * HEAD_DIM, (2 * g + 1) * 2 * HEAD_DIM)
        pair1 = slice((2 * g + 1) * 2 * HEAD_DIM, (2 * g + 2) * 2 * HEAD_DIM)
        for blk in range(ATTN_BLOCKS):
            rows = slice(blk * BLOCK, (blk + 1) * BLOCK)
            band = slice(blk * BLOCK, (blk + 2) * BLOCK)
            b_ref = bias0_ref if blk == 0 else bias_ref
            vd = v_all[band]
            q2 = jnp.concatenate([q_ref[rows, pair0], q_ref[rows, pair1]], axis=0)
            s_lo = lax.dot_general(q2, k_lo_all[band], contract_last, preferred_element_type=F32)
            s_hi = lax.dot_general(q2, k_hi_all[band], contract_last, preferred_element_type=F32)
            outs = {}
            for s, first_head in ((s_lo, 4 * g), (s_hi, 4 * g + 1)):
                for half in range(2):
                    head = first_head + 2 * half
                    sh = s[half * BLOCK:(half + 1) * BLOCK] + b_ref[0, head]
                    sink = sink_ref[head]
                    m = jnp.maximum(jnp.max(sh, axis=-1, keepdims=True), sink)
                    p = jnp.exp(sh - m)
                    den = jnp.sum(p, axis=-1, keepdims=True) + jnp.exp(sink - m)
                    pv = jnp.dot(p.astype(BF16), vd, preferred_element_type=F32)
                    outs[head] = pv / den
            o_ref[rows, pair0] = jnp.where(low_q, outs[4 * g], outs[4 * g + 1]).astype(BF16)
            o_ref[rows, pair1] = jnp.where(low_q, outs[4 * g + 2], outs[4 * g + 3]).astype(BF16)


def _attention(q, kd, vd, sinks, bias, batch, seq):
    t = q.shape[0]
    tq = ATTN_BLOCKS * BLOCK
    nt = seq // tq
    cur = lambda b, n: (b * nt + n, 0)
    prev = lambda b, n: ((b * nt + n) * ATTN_BLOCKS - jnp.minimum(n, 1), 0)
    bias_spec = lambda index_map: pl.BlockSpec((1, N_Q_HEADS, BLOCK, 2 * BLOCK), index_map)
    return pl.pallas_call(
        _attn_kernel,
        grid=(batch, nt),
        in_specs=[
            pl.BlockSpec(memory_space=pltpu.SMEM),
            pl.BlockSpec((tq, ATTN_WIDTH), cur),
            pl.BlockSpec((tq, 2 * KV_WIDTH), cur),
            pl.BlockSpec((BLOCK, 2 * KV_WIDTH), prev),
            pl.BlockSpec((tq, 2 * KV_WIDTH), cur),
            pl.BlockSpec((BLOCK, 2 * KV_WIDTH), prev),
            bias_spec(lambda b, n: (jnp.minimum(n, 1), 0, 0, 0)),
            bias_spec(lambda b, n: (1, 0, 0, 0)),
        ],
        out_specs=pl.BlockSpec((tq, ATTN_WIDTH), cur),
        out_shape=jax.ShapeDtypeStruct((t, ATTN_WIDTH), BF16),
        compiler_params=_params(("arbitrary", "arbitrary")),
        name="attention",
    )(sinks, q, kd, kd, vd, vd, bias, bias)


def _attention_bias():
    heads = jnp.arange(1, N_Q_HEADS + 1, dtype=F32)
    slopes = jnp.exp2(-8.0 * heads / N_Q_HEADS)
    dist = (jnp.arange(BLOCK)[:, None] + BLOCK) - jnp.arange(2 * BLOCK)[None, :]
    valid = (dist >= 0) & (dist < BLOCK)
    has_prev = jnp.arange(2 * BLOCK)[None, :] >= BLOCK
    valid = jnp.stack([valid & has_prev, valid])
    bias = -slopes[:, None, None] * dist.astype(F32)
    return jnp.where(valid[:, None], bias[None], -jnp.inf)


def _conv_kernel(tiles_per_seq, cur_ref, halo_ref, w_ref, b_ref, lng_ref, lnb_ref, o_ref, buf, ybuf):
    first = (pl.program_id(0) % tiles_per_seq) == 0
    groups = CONV_SEG_LEN // V7X_SUBLANES
    for s in range(CONV_SLABS):
        cols = slice(s * V7X_LANES, (s + 1) * V7X_LANES)
        for j in range(CONV_SEGS):
            for gi in range(groups):
                t0 = CONV_SEG_ROWS * j - CONV_HALO + V7X_SUBLANES * gi
                if t0 < 0:
                    h0 = CONV_HALO + t0
                    src = jnp.where(first, 0.0, halo_ref[h0:h0 + V7X_SUBLANES, cols])
                else:
                    src = cur_ref[t0:t0 + V7X_SUBLANES, cols]
                buf[s, pl.ds(CONV_SEGS * V7X_SUBLANES * gi + j, V7X_SUBLANES, stride=CONV_SEGS), :] = src

    base = CONV_HALO - (CONV_KERNEL - 1)
    out_groups = CONV_SEG_ROWS // V7X_SUBLANES

    def seg_body(it, carry):
        s = it // CONV_SEGS
        j = it % CONV_SEGS
        acc = [None] * out_groups
        for k in range(CONV_KERNEL):
            wk = w_ref[s, k]
            for gi in range(out_groups):
                i = V7X_SUBLANES * gi + base + k
                term = buf[s, pl.ds(CONV_SEGS * i + j, V7X_SUBLANES, stride=CONV_SEGS), :] * wk
                acc[gi] = term if acc[gi] is None else acc[gi] + term
        row0 = pl.multiple_of(j * CONV_SEG_ROWS, CONV_SEG_ROWS)
        ybuf[s, pl.ds(row0, CONV_SEG_ROWS), :] = jnp.concatenate(acc, axis=0)
        return carry

    lax.fori_loop(0, CONV_SLABS * CONV_SEGS, seg_body, 0)

    y = jnp.concatenate([ybuf[s] for s in range(CONV_SLABS)], axis=1) + b_ref[...]
    mu = jnp.mean(y, axis=-1, keepdims=True)
    yc = y - mu
    var = jnp.mean(yc * yc, axis=-1, keepdims=True)
    z = yc * lax.rsqrt(var + NORM_EPS) * lng_ref[...] + lnb_ref[...]
    o_ref[...] = (z * _sigmoid(z)).astype(BF16)


def _conv_weights(w):
    ws = w.reshape(CONV_KERNEL, CONV_SLABS, V7X_LANES).transpose(1, 0, 2)
    return jnp.broadcast_to(ws[:, :, None, :], (CONV_SLABS, CONV_KERNEL, V7X_SUBLANES, V7X_LANES))


def _conv(glu, w_slab, b, lng, lnb, seq):
    t = glu.shape[0]
    tc = TC_CONV
    ratio = tc // CONV_HALO
    fixed = lambda i: (0, 0)
    return pl.pallas_call(
        functools.partial(_conv_kernel, seq // tc),
        grid=(t // tc,),
        in_specs=[
            pl.BlockSpec((tc, CONV_WIDTH), lambda i: (i, 0)),
            pl.BlockSpec((CONV_HALO, CONV_WIDTH), lambda i: (jnp.maximum(i * ratio - 1, 0), 0)),
            pl.BlockSpec((CONV_SLABS, CONV_KERNEL, V7X_SUBLANES, V7X_LANES), lambda i: (0, 0, 0, 0)),
            pl.BlockSpec((1, CONV_WIDTH), fixed),
            pl.BlockSpec((1, CONV_WIDTH), fixed),
            pl.BlockSpec((1, CONV_WIDTH), fixed),
        ],
        out_specs=pl.BlockSpec((tc, CONV_WIDTH), lambda i: (i, 0)),
        out_shape=jax.ShapeDtypeStruct((t, CONV_WIDTH), BF16),
        scratch_shapes=[pltpu.VMEM((CONV_SLABS, CONV_SEGS * CONV_SEG_LEN, V7X_LANES), F32),
                        pltpu.VMEM((CONV_SLABS, tc, V7X_LANES), F32)],
        compiler_params=_params(("arbitrary",)),
        name="conv",
    )(glu, glu, w_slab, b, lng, lnb)


def _out_proj_residual(attn_ref, conv_ref, x_ref, w_ref, w_bf16):
    @pl.when(pl.program_id(0) == 0)
    def _():
        w_bf16[...] = w_ref[...].astype(BF16)

    y = (jnp.dot(attn_ref[...], w_bf16[:ATTN_WIDTH], preferred_element_type=F32)
         + jnp.dot(conv_ref[...], w_bf16[ATTN_WIDTH:], preferred_element_type=F32))
    return x_ref[...] + y


def _mixer_out_kernel(attn_ref, conv_ref, x_ref, w_ref, o_ref, w_bf16):
    o_ref[...] = _out_proj_residual(attn_ref, conv_ref, x_ref, w_ref, w_bf16)


def _mixer_out_router_kernel(attn_ref, conv_ref, x_ref, w_ref, g_ref, wr_hi_ref, wr_lo_ref,
                             o_ref, route_ref, count_ref, w_bf16, carry):
    i = pl.program_id(0)
    tm = x_ref.shape[0]

    @pl.when(i == 0)
    def _():
        carry[...] = jnp.zeros_like(carry)

    x = _out_proj_residual(attn_ref, conv_ref, x_ref, w_ref, w_bf16)
    o_ref[...] = x

    ms = jnp.mean(x * x, axis=-1, keepdims=True)
    h = x * lax.rsqrt(ms + NORM_EPS) * g_ref[...]
    h_hi = h.astype(BF16)
    h_lo = (h - h_hi.astype(F32)).astype(BF16)
    wr_hi = wr_hi_ref[...]
    logits = (jnp.dot(h_hi, wr_hi, preferred_element_type=F32)
              + jnp.dot(h_lo, wr_hi, preferred_element_type=F32)
              + jnp.dot(h_hi, wr_lo_ref[...], preferred_element_type=F32))

    lane = lax.broadcasted_iota(jnp.int32, (tm, V7X_LANES), 1)
    lane_f = lane.astype(F32)
    neg_inf = -jnp.inf
    lg = jnp.where(lane < N_EXPERTS, logits, neg_inf)
    m1 = jnp.max(lg, axis=-1, keepdims=True)
    i1 = jnp.min(jnp.where(lg == m1, lane_f, float(V7X_LANES)), axis=-1, keepdims=True)
    oh1 = lane_f == i1
    lg2 = jnp.where(oh1, neg_inf, lg)
    m2 = jnp.max(lg2, axis=-1, keepdims=True)
    i2 = jnp.min(jnp.where(lg2 == m2, lane_f, float(V7X_LANES)), axis=-1, keepdims=True)
    oh2 = lane_f == i2
    e = jnp.exp(m2 - m1)
    w1 = 1.0 / (1.0 + e)
    w2 = e / (1.0 + e)

    member = jnp.where(oh1 | oh2, 1.0, 0.0)
    r_id = lax.broadcasted_iota(jnp.int32, (tm, tm), 0)
    c_id = lax.broadcasted_iota(jnp.int32, (tm, tm), 1)
    tri = jnp.where(r_id > c_id, 1.0, 0.0).astype(BF16)
    before = jnp.dot(tri, member.astype(BF16), preferred_element_type=F32) + carry[...]
    rank1 = jnp.sum(jnp.where(oh1, before, 0.0), axis=-1, keepdims=True)
    rank2 = jnp.sum(jnp.where(oh2, before, 0.0), axis=-1, keepdims=True)
    carry[...] = carry[...] + jnp.sum(member, axis=0, keepdims=True)

    route = jnp.where(lane == 0, i1, 0.0)
    route = jnp.where(lane == 1, i2, route)
    route = jnp.where(lane == 2, w1, route)
    route = jnp.where(lane == 3, w2, route)
    route = jnp.where(lane == 4, rank1, route)
    route = jnp.where(lane == 5, rank2, route)
    route_ref[...] = route
    count_ref[...] = jnp.broadcast_to(carry[...], count_ref.shape)


def _mixer_out(attn, conv, x, w_out, layer, router=None):
    t = x.shape[0]
    tm = TM_PROJ
    row = lambda i: (i, 0)
    fixed = lambda i: (0, 0)
    in_specs = [
        pl.BlockSpec((tm, ATTN_WIDTH), row),
        pl.BlockSpec((tm, CONV_WIDTH), row),
        pl.BlockSpec((tm, D_MODEL), row),
        pl.BlockSpec((None, D_MODEL, D_MODEL), lambda i: (layer, 0, 0)),
    ]
    w_scratch = pltpu.VMEM((D_MODEL, D_MODEL), BF16)
    if router is None:
        return pl.pallas_call(
            _mixer_out_kernel,
            grid=(t // tm,),
            in_specs=in_specs,
            out_specs=pl.BlockSpec((tm, D_MODEL), row),
            out_shape=jax.ShapeDtypeStruct((t, D_MODEL), F32),
            scratch_shapes=[w_scratch],
            compiler_params=_params(("arbitrary",)),
            name="mixer_out",
        )(attn, conv, x, w_out)
    g, wr_hi, wr_lo = router
    return pl.pallas_call(
        _mixer_out_router_kernel,
        grid=(t // tm,),
        in_specs=in_specs + [
            pl.BlockSpec((1, D_MODEL), fixed),
            pl.BlockSpec((D_MODEL, V7X_LANES), fixed),
            pl.BlockSpec((D_MODEL, V7X_LANES), fixed),
        ],
        out_specs=[
            pl.BlockSpec((tm, D_MODEL), row),
            pl.BlockSpec((tm, V7X_LANES), row),
            pl.BlockSpec((V7X_SUBLANES, V7X_LANES), fixed),
        ],
        out_shape=[
            jax.ShapeDtypeStruct((t, D_MODEL), F32),
            jax.ShapeDtypeStruct((t, V7X_LANES), F32),
            jax.ShapeDtypeStruct((V7X_SUBLANES, V7X_LANES), F32),
        ],
        scratch_shapes=[w_scratch, pltpu.VMEM((1, V7X_LANES), F32)],
        compiler_params=_params(("arbitrary",)),
        name="mixer_out_router",
    )(attn, conv, x, w_out, g, wr_hi, wr_lo)


def _ffn_chunk_copies(w_hbm, lead, j, tf, stage, sem, slot):
    cols = pl.ds(j * tf, tf)
    every = slice(None)
    index = (lead + (every, cols), lead + (every, cols), lead + (cols, every))
    return [pltpu.make_async_copy(w_hbm[m].at[index[m]], stage[m].at[slot], sem.at[m, slot])
            for m in range(3)]


def _swiglu_chunk(hbf, res, j):
    h = hbf[...]
    gate = jnp.dot(h, res[0][j], preferred_element_type=F32)
    up = jnp.dot(h, res[1][j], preferred_element_type=F32)
    a = (gate * _sigmoid(gate) * up).astype(BF16)
    return jnp.dot(a, res[2][j], preferred_element_type=F32)


def _swiglu_tile(load_weights, hbf, w_hbm, lead, res, stage, sem, acc, nj, tf, emit):
    @pl.when(load_weights)
    def _():
        for c in _ffn_chunk_copies(w_hbm, lead, 0, tf, stage, sem, 0):
            c.start()
        for j in range(nj):
            slot = j % 2
            if j + 1 < nj:
                for c in _ffn_chunk_copies(w_hbm, lead, j + 1, tf, stage, sem, 1 - slot):
                    c.start()
            for c in _ffn_chunk_copies(w_hbm, lead, j, tf, stage, sem, slot):
                c.wait()
            for m in range(3):
                res[m][j] = stage[m][slot].astype(BF16)
            yj = _swiglu_chunk(hbf, res, j)
            if j == 0:
                acc[...] = yj
            else:
                acc[...] += yj
        emit(acc[...])

    @pl.when(jnp.logical_not(load_weights))
    def _():
        y = _swiglu_chunk(hbf, res, 0)
        for j in range(1, nj):
            y = y + _swiglu_chunk(hbf, res, j)
        emit(y)


def _ffn_scratch(tm, nj, tf):
    return [
        pltpu.VMEM((nj, D_MODEL, tf), BF16),
        pltpu.VMEM((nj, D_MODEL, tf), BF16),
        pltpu.VMEM((nj, tf, D_MODEL), BF16),
        pltpu.VMEM((2, D_MODEL, tf), F32),
        pltpu.VMEM((2, D_MODEL, tf), F32),
        pltpu.VMEM((2, tf, D_MODEL), F32),
        pltpu.SemaphoreType.DMA((3, 2)),
        pltpu.VMEM((tm, D_MODEL), BF16),
        pltpu.VMEM((tm, D_MODEL), F32),
    ]


def _dense_ffn_kernel(layer, nj, tf, x_ref, g_ref, wg, wu, wd, o_ref,
                      rg, ru, rd, sg, su, sd, sem, hbf, acc):
    hbf[...] = _rms_norm_bf16(x_ref[...], g_ref[...])

    def emit(y):
        o_ref[...] = x_ref[...] + y

    _swiglu_tile(pl.program_id(0) == 0, hbf, (wg, wu, wd), (layer,), (rg, ru, rd),
                 (sg, su, sd), sem, acc, nj, tf, emit)


def _dense_ffn(x, g, wg, wu, wd, layer):
    t = x.shape[0]
    tm = TM_FFN
    tf = TF_DENSE
    nj = wg.shape[2] // tf
    hbm = pl.BlockSpec(memory_space=pl.ANY)
    return pl.pallas_call(
        functools.partial(_dense_ffn_kernel, layer, nj, tf),
        grid=(t // tm,),
        in_specs=[
            pl.BlockSpec((tm, D_MODEL), lambda i: (i, 0)),
            pl.BlockSpec((1, D_MODEL), lambda i: (0, 0)),
            hbm, hbm, hbm,
        ],
        out_specs=pl.BlockSpec((tm, D_MODEL), lambda i: (i, 0)),
        out_shape=jax.ShapeDtypeStruct((t, D_MODEL), F32),
        scratch_shapes=_ffn_scratch(tm, nj, tf),
        compiler_params=_params(("arbitrary",)),
        name="dense_ffn",
    )(x, g, wg, wu, wd)


def _token_rows(ref, row):
    return ref.at[pl.ds(pl.multiple_of(row * ROW_CHUNKS, ROW_CHUNKS), ROW_CHUNKS), :]


def _dispatch_kernel(pos_ref, x_ref, g_ref, xs_in_ref, xs_ref, buf, sem):
    del xs_in_ref
    tm = x_ref.shape[0]
    x = x_ref[...]
    ms = jnp.mean(x * x, axis=-1, keepdims=True)
    h = x * lax.rsqrt(ms + NORM_EPS) * g_ref[...]
    for c in range(ROW_CHUNKS):
        buf[pl.ds(c, tm, stride=ROW_CHUNKS), :] = h[:, c * V7X_LANES:(c + 1) * V7X_LANES]

    def issue(r, carry):
        src = _token_rows(buf, r)
        for k in range(2):
            dst = _token_rows(xs_ref, pos_ref[0, 0, 2 * r + k])
            pltpu.make_async_copy(src, dst, sem).start(priority=k)
        return carry

    lax.fori_loop(0, tm, issue, 0)
    for k in range(2):
        pltpu.make_async_copy(buf, xs_ref.at[pl.ds(0, tm * ROW_CHUNKS), :], sem).wait()


def _dispatch(pos_tiles, x, g, n_rows):
    t = x.shape[0]
    tm = TM_TOK
    zeros = jnp.zeros((n_rows * ROW_CHUNKS, V7X_LANES), F32)
    return pl.pallas_call(
        _dispatch_kernel,
        grid=(t // tm,),
        in_specs=[
            pl.BlockSpec((1, 1, 2 * tm), lambda i: (i, 0, 0), memory_space=pltpu.SMEM),
            pl.BlockSpec((tm, D_MODEL), lambda i: (i, 0)),
            pl.BlockSpec((1, D_MODEL), lambda i: (0, 0)),
            pl.BlockSpec(memory_space=pl.ANY),
        ],
        out_specs=pl.BlockSpec(memory_space=pl.ANY),
        out_shape=jax.ShapeDtypeStruct(zeros.shape, F32),
        scratch_shapes=[pltpu.VMEM((tm * ROW_CHUNKS, V7X_LANES), F32),
                        pltpu.SemaphoreType.DMA(())],
        input_output_aliases={3: 0},
        compiler_params=_params(("arbitrary",)),
        name="moe_dispatch",
    )(pos_tiles, x, g, zeros)


def _moe_ffn_kernel(layer, nj, tf, te_ref, tr_ref, tl_ref, xs_ref, wg, wu, wd, ys_ref,
                    rg, ru, rd, sg, su, sd, sem, hbf, acc):
    i = pl.program_id(0)
    tm = hbf.shape[0]
    rows = tr_ref[i]

    @pl.when(rows > 0)
    def _():
        for c in range(ROW_CHUNKS):
            hbf[:, c * V7X_LANES:(c + 1) * V7X_LANES] = (
                xs_ref[pl.ds(c, tm, stride=ROW_CHUNKS), :].astype(BF16))

        def emit(y):
            for c in range(ROW_CHUNKS):
                ys_ref[pl.ds(c, tm, stride=ROW_CHUNKS), :] = y[:, c * V7X_LANES:(c + 1) * V7X_LANES]

        _swiglu_tile(tl_ref[i] != 0, hbf, (wg, wu, wd), (layer, te_ref[i]), (rg, ru, rd),
                     (sg, su, sd), sem, acc, nj, tf, emit)

    @pl.when(rows == 0)
    def _():
        ys_ref[...] = jnp.zeros_like(ys_ref)


def _moe_ffn(tile_expert, tile_rows, tile_load, xs, wg, wu, wd, layer):
    n_tiles = tile_expert.shape[0]
    tm = TM_MOE
    tf = TF_MOE
    nj = wg.shape[3] // tf
    hbm = pl.BlockSpec(memory_space=pl.ANY)
    grid_spec = pltpu.PrefetchScalarGridSpec(
        num_scalar_prefetch=3,
        grid=(n_tiles,),
        in_specs=[
            pl.BlockSpec((tm * ROW_CHUNKS, V7X_LANES), lambda i, te, tr, tl: (i, 0)),
            hbm, hbm, hbm,
        ],
        out_specs=pl.BlockSpec((tm * ROW_CHUNKS, V7X_LANES), lambda i, te, tr, tl: (i, 0)),
        scratch_shapes=_ffn_scratch(tm, nj, tf),
    )
    return pl.pallas_call(
        functools.partial(_moe_ffn_kernel, layer, nj, tf),
        grid_spec=grid_spec,
        out_shape=jax.ShapeDtypeStruct(xs.shape, F32),
        compiler_params=_params(("arbitrary",)),
        name="moe_ffn",
    )(tile_expert, tile_rows, tile_load, xs, wg, wu, wd)


def _combine_kernel(pos_ref, x_ref, route_ref, ys_ref, o_ref, y1buf, y2buf, sem):
    tm = x_ref.shape[0]

    def issue(r, carry):
        for k, ybuf in ((0, y1buf), (1, y2buf)):
            src = _token_rows(ys_ref, pos_ref[0, 0, 2 * r + k])
            pltpu.make_async_copy(src, _token_rows(ybuf, r), sem).start(priority=k)
        return carry

    lax.fori_loop(0, tm, issue, 0)
    for ybuf in (y1buf, y2buf):
        pltpu.make_async_copy(ys_ref.at[pl.ds(0, tm * ROW_CHUNKS), :], ybuf, sem).wait()

    w1 = route_ref[:, 2:3]
    w2 = route_ref[:, 3:4]
    for c in range(ROW_CHUNKS):
        cols = slice(c * V7X_LANES, (c + 1) * V7X_LANES)
        y1 = y1buf[pl.ds(c, tm, stride=ROW_CHUNKS), :]
        y2 = y2buf[pl.ds(c, tm, stride=ROW_CHUNKS), :]
        o_ref[:, cols] = x_ref[:, cols] + (w1 * y1 + w2 * y2)


def _combine(pos_tiles, x, route, ys):
    t = x.shape[0]
    tm = TM_TOK
    return pl.pallas_call(
        _combine_kernel,
        grid=(t // tm,),
        in_specs=[
            pl.BlockSpec((1, 1, 2 * tm), lambda i: (i, 0, 0), memory_space=pltpu.SMEM),
            pl.BlockSpec((tm, D_MODEL), lambda i: (i, 0)),
            pl.BlockSpec((tm, V7X_LANES), lambda i: (i, 0)),
            pl.BlockSpec(memory_space=pl.ANY),
        ],
        out_specs=pl.BlockSpec((tm, D_MODEL), lambda i: (i, 0)),
        out_shape=jax.ShapeDtypeStruct((t, D_MODEL), F32),
        scratch_shapes=[pltpu.VMEM((tm * ROW_CHUNKS, V7X_LANES), F32),
                        pltpu.VMEM((tm * ROW_CHUNKS, V7X_LANES), F32),
                        pltpu.SemaphoreType.DMA(())],
        compiler_params=_params(("arbitrary",)),
        name="moe_combine",
    )(pos_tiles, x, route, ys)


def _moe_layer(x, route, counts, g, wg, wu, wd, layer):
    t = x.shape[0]
    tm = TM_MOE
    n_tiles = (2 * t) // tm + N_EXPERTS
    expert = route[:, 0:2].astype(jnp.int32)
    rank = route[:, 4:6].astype(jnp.int32)
    count = counts[0, :N_EXPERTS].astype(jnp.int32)
    padded = ((count + tm - 1) // tm) * tm
    gend = jnp.cumsum(padded)
    gstart = gend - padded
    ids = jnp.arange(N_EXPERTS, dtype=jnp.int32)
    pos = rank + jnp.sum(jnp.where(expert[..., None] == ids, gstart, 0), axis=-1)
    tile_start = jnp.arange(n_tiles, dtype=jnp.int32) * tm
    last_expert = jnp.max(jnp.where(padded > 0, ids, 0))
    tile_expert = jnp.minimum(
        jnp.sum((tile_start[:, None] >= gend[None, :]).astype(jnp.int32), axis=1), last_expert)
    used_end = (gstart + count)[tile_expert]
    tile_rows = jnp.clip(used_end - tile_start, 0, tm)
    prev_expert = jnp.concatenate([jnp.full((1,), -1, jnp.int32), tile_expert[:-1]])
    tile_load = ((tile_rows > 0) & (tile_expert != prev_expert)).astype(jnp.int32)
    pos_tiles = pos.reshape(t // TM_TOK, 1, 2 * TM_TOK)

    xs = _dispatch(pos_tiles, x, g, n_tiles * tm)
    ys = _moe_ffn(tile_expert, tile_rows, tile_load, xs, wg, wu, wd, layer)
    return _combine(pos_tiles, x, route, ys)


def kernel(x, attn_norm_g, w_in, q_norm_g, k_norm_g, sinks, conv_w, conv_b, conv_ln_g,
           conv_ln_b, w_out, ffn_norm_g, dense_w_gate, dense_w_up, dense_w_down,
           w_router, moe_w_gate, moe_w_up, moe_w_down):
    batch, seq, d = x.shape
    depth = w_in.shape[0]
    t = batch * seq
    assert d == D_MODEL and seq % TC_CONV == 0 and t % TM_PROJ == 0 and t % TM_TOK == 0
    assert seq % (ATTN_BLOCKS * BLOCK) == 0
    assert w_in.shape[2] == IN_PROJ_WIDTH and conv_w.shape[1] == CONV_KERNEL
    assert dense_w_gate.shape[2] % TF_DENSE == 0 and moe_w_gate.shape[3] % TF_MOE == 0

    xt = x.reshape(t, d)
    bias = _attention_bias()
    head_id = jnp.arange(SEG_WIDTH) // HEAD_DIM
    seg = (head_id[:, None] == head_id[None, :]).astype(BF16)
    scale = HEAD_DIM ** -0.5

    for layer in range(depth):
        gain = jnp.concatenate([jnp.tile(q_norm_g[layer], N_Q_HEADS) * scale,
                                jnp.tile(k_norm_g[layer], N_KV_HEADS)])[None, :]
        q, kd, vd, glu = _mixer_in(xt, attn_norm_g[layer][None, :], w_in, layer, gain, seg)
        attn = _attention(q, kd, vd, sinks[layer], bias, batch, seq)
        conv = _conv(glu, _conv_weights(conv_w[layer]), conv_b[layer][None, :],
                     conv_ln_g[layer][None, :],
                     conv_ln_b[layer][None, :], seq)
        ffn_g = ffn_norm_g[layer][None, :]
        i = layer // 2
        if layer % 2 == 0:
            xm = _mixer_out(attn, conv, xt, w_out, layer)
            xt = _dense_ffn(xm, ffn_g, dense_w_gate, dense_w_up, dense_w_down, i)
        else:
            wr = jnp.zeros((D_MODEL, V7X_LANES), F32).at[:, :N_EXPERTS].set(w_router[i])
            wr_hi = wr.astype(BF16)
            wr_lo = (wr - wr_hi.astype(F32)).astype(BF16)
            xm, route, counts = _mixer_out(attn, conv, xt, w_out, layer,
                                           router=(ffn_g, wr_hi, wr_lo))
            xt = _moe_layer(xm, route, counts, ffn_g, moe_w_gate, moe_w_up, moe_w_down, i)
    return xt.reshape(batch, seq, d)
```

```python
import functools

import jax
import jax.numpy as jnp
from jax import lax
from jax.experimental import pallas as pl
from jax.experimental.pallas import tpu as pltpu

F32 = jnp.float32
BF16 = jnp.bfloat16

D_MODEL = 1024
HEAD_DIM = 64
N_Q_HEADS = 8
N_KV_HEADS = 2
GQA_GROUP = N_Q_HEADS // N_KV_HEADS
ATTN_WIDTH = N_Q_HEADS * HEAD_DIM
KV_WIDTH = N_KV_HEADS * HEAD_DIM
QK_WIDTH = ATTN_WIDTH + KV_WIDTH
SEG_WIDTH = 256
BLOCK = 128
CONV_WIDTH = D_MODEL - ATTN_WIDTH
CONV_KERNEL = 31
IN_PROJ_WIDTH = ATTN_WIDTH + 2 * KV_WIDTH + 2 * CONV_WIDTH
N_EXPERTS = 8
NORM_EPS = 1e-6
LOG2_E = 1.4426950408889634

V7X_LANES = 128
V7X_SUBLANES = 8
V7X_VMEM_LIMIT_BYTES = 56 * 1024 * 1024
ROW_CHUNKS = D_MODEL // V7X_LANES

TM_PROJ = 512
ATTN_BLOCKS = 4
TC_CONV = 256
CONV_HALO = 32
CONV_SLABS = CONV_WIDTH // V7X_LANES
CONV_SEGS = 4
CONV_SEG_ROWS = TC_CONV // CONV_SEGS
CONV_SEG_LEN = CONV_SEG_ROWS + CONV_HALO
TM_FFN = 512
TF_DENSE = 256
TM_MOE = 512
TF_MOE = 512
TM_TOK = 512


def _params(sem, vmem=V7X_VMEM_LIMIT_BYTES):
    return pltpu.CompilerParams(dimension_semantics=sem, vmem_limit_bytes=vmem)


def _sigmoid(x):
    return 1.0 / (1.0 + jnp.exp(-x))


def _rms_norm_bf16(x, g):
    ms = jnp.mean(x * x, axis=-1, keepdims=True)
    return (x * lax.rsqrt(ms + NORM_EPS) * g).astype(BF16)


def _mixer_in_kernel(x_ref, g_ref, w_ref, gain_ref, seg_ref, q_ref, k_ref, v_ref, glu_ref, w_bf16):
    @pl.when(pl.program_id(0) == 0)
    def _():
        w_bf16[...] = w_ref[...].astype(BF16)

    h = _rms_norm_bf16(x_ref[...], g_ref[...])
    proj = jnp.dot(h, w_bf16[...], preferred_element_type=F32)
    qk = proj[:, :QK_WIDTH]
    sq = qk * qk
    hi = sq.astype(BF16)
    lo = (sq - hi.astype(F32)).astype(BF16)
    seg = seg_ref[...]
    parts = []
    for c0 in range(0, QK_WIDTH, SEG_WIDTH):
        width = min(SEG_WIDTH, QK_WIDTH - c0)
        blk = slice(c0, c0 + width)
        parts.append(jnp.dot(hi[:, blk], seg[:width, :width], preferred_element_type=F32)
                     + jnp.dot(lo[:, blk], seg[:width, :width], preferred_element_type=F32))
    ss = jnp.concatenate(parts, axis=1)
    qkn = qk * lax.rsqrt(ss * (1.0 / HEAD_DIM) + NORM_EPS) * gain_ref[...]
    q_ref[...] = qkn[:, :ATTN_WIDTH].astype(BF16)
    lane = lax.broadcasted_iota(jnp.int32, (x_ref.shape[0], KV_WIDTH), 1)
    low = lane < HEAD_DIM
    for src, dst in ((qkn[:, ATTN_WIDTH:QK_WIDTH], k_ref),
                     (proj[:, QK_WIDTH:QK_WIDTH + KV_WIDTH], v_ref)):
        rot = pltpu.roll(src, HEAD_DIM, axis=1)
        dst[:, :KV_WIDTH] = jnp.where(low, src, rot).astype(BF16)
        dst[:, KV_WIDTH:] = jnp.where(low, rot, src).astype(BF16)
    o_c = QK_WIDTH + KV_WIDTH
    u = proj[:, o_c:o_c + CONV_WIDTH]
    gate = proj[:, o_c + CONV_WIDTH:]
    glu_ref[...] = u * _sigmoid(gate)


def _mixer_in(x, g, w_in, layer, gain, seg):
    t = x.shape[0]
    tm = TM_PROJ
    row = lambda i: (i, 0)
    fixed = lambda i: (0, 0)
    return pl.pallas_call(
        _mixer_in_kernel,
        grid=(t // tm,),
        in_specs=[
            pl.BlockSpec((tm, D_MODEL), row),
            pl.BlockSpec((1, D_MODEL), fixed),
            pl.BlockSpec((None, D_MODEL, IN_PROJ_WIDTH), lambda i: (layer, 0, 0)),
            pl.BlockSpec((1, QK_WIDTH), fixed),
            pl.BlockSpec((SEG_WIDTH, SEG_WIDTH), fixed),
        ],
        out_specs=[
            pl.BlockSpec((tm, ATTN_WIDTH), row),
            pl.BlockSpec((tm, 2 * KV_WIDTH), row),
            pl.BlockSpec((tm, 2 * KV_WIDTH), row),
            pl.BlockSpec((tm, CONV_WIDTH), row),
        ],
        out_shape=[
            jax.ShapeDtypeStruct((t, ATTN_WIDTH), BF16),
            jax.ShapeDtypeStruct((t, 2 * KV_WIDTH), BF16),
            jax.ShapeDtypeStruct((t, 2 * KV_WIDTH), BF16),
            jax.ShapeDtypeStruct((t, CONV_WIDTH), F32),
        ],
        scratch_shapes=[pltpu.VMEM((D_MODEL, IN_PROJ_WIDTH), BF16)],
        compiler_params=_params(("arbitrary",)),
        name="mixer_in",
    )(x, g, w_in, gain, seg)


def _attn_kernel(q_ref, kc_ref, kp_ref, vc_ref, vp_ref, bias0_ref, bias_ref, o_ref):
    n_keys = (ATTN_BLOCKS + 1) * BLOCK
    low = lax.broadcasted_iota(jnp.int32, (n_keys, 2 * HEAD_DIM), 1) < HEAD_DIM
    low_q = lax.broadcasted_iota(jnp.int32, (BLOCK, 2 * HEAD_DIM), 1) < HEAD_DIM
    key0 = lax.broadcasted_iota(jnp.int32, (2 * BLOCK, 2 * HEAD_DIM), 0) == 0
    zero = jnp.zeros((), BF16)
    contract_last = (((1,), (1,)), ((), ()))
    for g in range(N_KV_HEADS):
        cols = slice(g * 2 * HEAD_DIM, (g + 1) * 2 * HEAD_DIM)
        k_all = jnp.concatenate([kp_ref[:, cols], kc_ref[:, cols]], axis=0)
        v_all = jnp.concatenate([vp_ref[:, cols], vc_ref[:, cols]], axis=0)
        k_lo_all = jnp.where(low, k_all, zero)
        k_hi_all = jnp.where(low, zero, k_all)
        pair0 = slice((2 * g) * 2 * HEAD_DIM, (2 * g + 1) * 2 * HEAD_DIM)
        pair1 = slice((2 * g + 1) * 2 * HEAD_DIM, (2 * g + 2) * 2 * HEAD_DIM)
        for blk in range(ATTN_BLOCKS):
            rows = slice(blk * BLOCK, (blk + 1) * BLOCK)
            band = slice(blk * BLOCK, (blk + 2) * BLOCK)
            b_ref = bias0_ref if blk == 0 else bias_ref
            k_lo = jnp.where(key0, zero, k_lo_all[band])
            k_hi = jnp.where(key0, zero, k_hi_all[band])
            v_band = jnp.where(key0, zero, v_all[band])
            v_ext = jnp.concatenate([v_band, jnp.ones_like(v_band)], axis=1)
            q2 = jnp.concatenate([q_ref[rows, pair0], q_ref[rows, pair1]], axis=0)
            s_lo = lax.dot_general(q2, k_lo, contract_last, preferred_element_type=F32)
            s_hi = lax.dot_general(q2, k_hi, contract_last, preferred_element_type=F32)
            sh = jnp.concatenate([s_lo, s_hi], axis=0) + b_ref[0, g]
            m = jnp.max(sh, axis=-1, keepdims=True)
            p = jnp.exp2(sh - m).astype(BF16)
            pv = jnp.dot(p, v_ext, preferred_element_type=F32)
            out = pv[:, :2 * HEAD_DIM] / pv[:, 2 * HEAD_DIM:]
            o_ref[rows, pair0] = jnp.where(low_q, out[:BLOCK], out[2 * BLOCK:3 * BLOCK]).astype(BF16)
            o_ref[rows, pair1] = jnp.where(low_q, out[BLOCK:2 * BLOCK], out[3 * BLOCK:]).astype(BF16)


def _attention(q, kd, vd, bias, batch, seq):
    t = q.shape[0]
    tq = ATTN_BLOCKS * BLOCK
    nt = seq // tq
    cur = lambda b, n: (b * nt + n, 0)
    prev = lambda b, n: ((b * nt + n) * ATTN_BLOCKS - jnp.minimum(n, 1), 0)
    bias_spec = lambda index_map: pl.BlockSpec(
        (1, N_KV_HEADS, GQA_GROUP * BLOCK, 2 * BLOCK), index_map)
    return pl.pallas_call(
        _attn_kernel,
        grid=(batch, nt),
        in_specs=[
            pl.BlockSpec((tq, ATTN_WIDTH), cur),
            pl.BlockSpec((tq, 2 * KV_WIDTH), cur),
            pl.BlockSpec((BLOCK, 2 * KV_WIDTH), prev),
            pl.BlockSpec((tq, 2 * KV_WIDTH), cur),
            pl.BlockSpec((BLOCK, 2 * KV_WIDTH), prev),
            bias_spec(lambda b, n: (jnp.minimum(n, 1), 0, 0, 0)),
            bias_spec(lambda b, n: (1, 0, 0, 0)),
        ],
        out_specs=pl.BlockSpec((tq, ATTN_WIDTH), cur),
        out_shape=jax.ShapeDtypeStruct((t, ATTN_WIDTH), BF16),
        compiler_params=_params(("arbitrary", "arbitrary")),
        name="attention",
    )(q, kd, kd, vd, vd, bias, bias)


def _attention_bias(sinks):
    heads = jnp.arange(1, N_Q_HEADS + 1, dtype=F32)
    slopes = jnp.exp2(-8.0 * heads / N_Q_HEADS)
    key = jnp.arange(2 * BLOCK)[None, :]
    dist = (jnp.arange(BLOCK)[:, None] + BLOCK) - key
    valid = (dist >= 0) & (dist < BLOCK)
    valid = jnp.stack([valid & (key >= BLOCK), valid])
    bias = -slopes[:, None, None] * dist.astype(F32)
    bias = jnp.where(valid[:, None], bias[None], -jnp.inf)
    bias = jnp.where(key == 0, sinks[None, :, None, None], bias) * LOG2_E
    order = jnp.array([[4 * g, 4 * g + 2, 4 * g + 1, 4 * g + 3] for g in range(N_KV_HEADS)])
    return bias[:, order].reshape(2, N_KV_HEADS, GQA_GROUP * BLOCK, 2 * BLOCK)


def _conv_kernel(tiles_per_seq, cur_ref, halo_ref, w_ref, b_ref, lng_ref, lnb_ref, o_ref, buf, ybuf):
    first = (pl.program_id(0) % tiles_per_seq) == 0
    groups = CONV_SEG_LEN // V7X_SUBLANES
    for s in range(CONV_SLABS):
        cols = slice(s * V7X_LANES, (s + 1) * V7X_LANES)
        for j in range(CONV_SEGS):
            for gi in range(groups):
                t0 = CONV_SEG_ROWS * j - CONV_HALO + V7X_SUBLANES * gi
                if t0 < 0:
                    h0 = CONV_HALO + t0
                    src = jnp.where(first, 0.0, halo_ref[h0:h0 + V7X_SUBLANES, cols])
                else:
                    src = cur_ref[t0:t0 + V7X_SUBLANES, cols]
                buf[s, pl.ds(CONV_SEGS * V7X_SUBLANES * gi + j, V7X_SUBLANES, stride=CONV_SEGS), :] = src

    base = CONV_HALO - (CONV_KERNEL - 1)
    out_groups = CONV_SEG_ROWS // V7X_SUBLANES

    def seg_body(it, carry):
        s = it // CONV_SEGS
        j = it % CONV_SEGS
        acc = [None] * out_groups
        for k in range(CONV_KERNEL):
            wk = w_ref[s, k]
            for gi in range(out_groups):
                i = V7X_SUBLANES * gi + base + k
                term = buf[s, pl.ds(CONV_SEGS * i + j, V7X_SUBLANES, stride=CONV_SEGS), :] * wk
                acc[gi] = term if acc[gi] is None else acc[gi] + term
        row0 = pl.multiple_of(j * CONV_SEG_ROWS, CONV_SEG_ROWS)
        ybuf[s, pl.ds(row0, CONV_SEG_ROWS), :] = jnp.concatenate(acc, axis=0)
        return carry

    lax.fori_loop(0, CONV_SLABS * CONV_SEGS, seg_body, 0, unroll=2)

    y = jnp.concatenate([ybuf[s] for s in range(CONV_SLABS)], axis=1) + b_ref[...]
    mu = jnp.mean(y, axis=-1, keepdims=True)
    yc = y - mu
    var = jnp.mean(yc * yc, axis=-1, keepdims=True)
    z = yc * lax.rsqrt(var + NORM_EPS) * lng_ref[...] + lnb_ref[...]
    o_ref[...] = (z * _sigmoid(z)).astype(BF16)


def _conv_weights(w):
    ws = w.reshape(CONV_KERNEL, CONV_SLABS, V7X_LANES).transpose(1, 0, 2)
    return jnp.broadcast_to(ws[:, :, None, :], (CONV_SLABS, CONV_KERNEL, V7X_SUBLANES, V7X_LANES))


def _conv(glu, w_slab, b, lng, lnb, seq):
    t = glu.shape[0]
    tc = TC_CONV
    ratio = tc // CONV_HALO
    fixed = lambda i: (0, 0)
    return pl.pallas_call(
        functools.partial(_conv_kernel, seq // tc),
        grid=(t // tc,),
        in_specs=[
            pl.BlockSpec((tc, CONV_WIDTH), lambda i: (i, 0)),
            pl.BlockSpec((CONV_HALO, CONV_WIDTH), lambda i: (jnp.maximum(i * ratio - 1, 0), 0)),
            pl.BlockSpec((CONV_SLABS, CONV_KERNEL, V7X_SUBLANES, V7X_LANES), lambda i: (0, 0, 0, 0)),
            pl.BlockSpec((1, CONV_WIDTH), fixed),
            pl.BlockSpec((1, CONV_WIDTH), fixed),
            pl.BlockSpec((1, CONV_WIDTH), fixed),
        ],
        out_specs=pl.BlockSpec((tc, CONV_WIDTH), lambda i: (i, 0)),
        out_shape=jax.ShapeDtypeStruct((t, CONV_WIDTH), BF16),
        scratch_shapes=[pltpu.VMEM((CONV_SLABS, CONV_SEGS * CONV_SEG_LEN, V7X_LANES), F32),
                        pltpu.VMEM((CONV_SLABS, tc, V7X_LANES), F32)],
        compiler_params=_params(("arbitrary",)),
        name="conv",
    )(glu, glu, w_slab, b, lng, lnb)


def _out_proj_residual(attn_ref, conv_ref, x_ref, w_ref, w_bf16):
    @pl.when(pl.program_id(0) == 0)
    def _():
        w_bf16[...] = w_ref[...].astype(BF16)

    y = (jnp.dot(attn_ref[...], w_bf16[:ATTN_WIDTH], preferred_element_type=F32)
         + jnp.dot(conv_ref[...], w_bf16[ATTN_WIDTH:], preferred_element_type=F32))
    return x_ref[...] + y


def _mixer_out_kernel(attn_ref, conv_ref, x_ref, w_ref, o_ref, w_bf16):
    o_ref[...] = _out_proj_residual(attn_ref, conv_ref, x_ref, w_ref, w_bf16)


def _mixer_out_router_kernel(attn_ref, conv_ref, x_ref, w_ref, g_ref, wr_ref,
                             o_ref, route_ref, count_ref, w_bf16, tri, carry):
    i = pl.program_id(0)
    tm = x_ref.shape[0]

    @pl.when(i == 0)
    def _():
        carry[...] = jnp.zeros_like(carry)
        earlier = (lax.broadcasted_iota(jnp.int32, (tm, tm), 0)
                   < lax.broadcasted_iota(jnp.int32, (tm, tm), 1))
        tri[...] = jnp.where(earlier, 1.0, 0.0).astype(BF16)

    x = _out_proj_residual(attn_ref, conv_ref, x_ref, w_ref, w_bf16)
    o_ref[...] = x

    ms = jnp.mean(x * x, axis=-1, keepdims=True)
    h = x * lax.rsqrt(ms + NORM_EPS) * g_ref[...]
    h_hi = h.astype(BF16)
    h_lo = (h - h_hi.astype(F32)).astype(BF16)
    both = jnp.dot(jnp.concatenate([h_hi, h_lo], axis=1), wr_ref[...],
                   preferred_element_type=F32)
    logits = both[:, :V7X_LANES] + both[:, V7X_LANES:]

    lt = jnp.transpose(logits)[:N_EXPERTS]
    sub = lax.broadcasted_iota(jnp.int32, (N_EXPERTS, tm), 0)
    sub_f = sub.astype(F32)
    neg_inf = -jnp.inf
    none = float(N_EXPERTS)
    m1 = jnp.max(lt, axis=0, keepdims=True)
    i1 = jnp.min(jnp.where(lt == m1, sub_f, none), axis=0, keepdims=True)
    oh1 = sub_f == i1
    lt2 = jnp.where(oh1, neg_inf, lt)
    m2 = jnp.max(lt2, axis=0, keepdims=True)
    i2 = jnp.min(jnp.where(lt2 == m2, sub_f, none), axis=0, keepdims=True)
    oh2 = sub_f == i2
    e = jnp.exp(m2 - m1)
    w1 = 1.0 / (1.0 + e)
    w2 = e / (1.0 + e)

    member = jnp.where(oh1 | oh2, 1.0, 0.0)
    before = (jnp.dot(member.astype(BF16), tri[...], preferred_element_type=F32)
              + carry[:, 0:1])
    rank1 = jnp.sum(jnp.where(oh1, before, 0.0), axis=0, keepdims=True)
    rank2 = jnp.sum(jnp.where(oh2, before, 0.0), axis=0, keepdims=True)
    carry[...] = carry[...] + jnp.sum(member, axis=1, keepdims=True)

    route = jnp.where(sub == 0, i1, 0.0)
    route = jnp.where(sub == 1, i2, route)
    route = jnp.where(sub == 2, w1, route)
    route = jnp.where(sub == 3, w2, route)
    route = jnp.where(sub == 4, rank1, route)
    route = jnp.where(sub == 5, rank2, route)
    route_ref[0] = route
    count_ref[...] = carry[...]


def _mixer_out(attn, conv, x, w_out, layer, router=None):
    t = x.shape[0]
    tm = TM_PROJ
    row = lambda i: (i, 0)
    fixed = lambda i: (0, 0)
    in_specs = [
        pl.BlockSpec((tm, ATTN_WIDTH), row),
        pl.BlockSpec((tm, CONV_WIDTH), row),
        pl.BlockSpec((tm, D_MODEL), row),
        pl.BlockSpec((None, D_MODEL, D_MODEL), lambda i: (layer, 0, 0)),
    ]
    w_scratch = pltpu.VMEM((D_MODEL, D_MODEL), BF16)
    if router is None:
        return pl.pallas_call(
            _mixer_out_kernel,
            grid=(t // tm,),
            in_specs=in_specs,
            out_specs=pl.BlockSpec((tm, D_MODEL), row),
            out_shape=jax.ShapeDtypeStruct((t, D_MODEL), F32),
            scratch_shapes=[w_scratch],
            compiler_params=_params(("arbitrary",)),
            name="mixer_out",
        )(attn, conv, x, w_out)
    g, wr_split = router
    return pl.pallas_call(
        _mixer_out_router_kernel,
        grid=(t // tm,),
        in_specs=in_specs + [
            pl.BlockSpec((1, D_MODEL), fixed),
            pl.BlockSpec((2 * D_MODEL, 2 * V7X_LANES), fixed),
        ],
        out_specs=[
            pl.BlockSpec((tm, D_MODEL), row),
            pl.BlockSpec((1, N_EXPERTS, tm), lambda i: (i, 0, 0)),
            pl.BlockSpec((N_EXPERTS, V7X_LANES), fixed),
        ],
        out_shape=[
            jax.ShapeDtypeStruct((t, D_MODEL), F32),
            jax.ShapeDtypeStruct((t // tm, N_EXPERTS, tm), F32),
            jax.ShapeDtypeStruct((N_EXPERTS, V7X_LANES), F32),
        ],
        scratch_shapes=[w_scratch, pltpu.VMEM((tm, tm), BF16),
                        pltpu.VMEM((N_EXPERTS, V7X_LANES), F32)],
        compiler_params=_params(("arbitrary",)),
        name="mixer_out_router",
    )(attn, conv, x, w_out, g, wr_split)


def _ffn_chunk_copies(w_hbm, lead, j, tf, stage, sem, slot):
    cols = pl.ds(j * tf, tf)
    every = slice(None)
    index = (lead + (every, cols), lead + (every, cols), lead + (cols, every))
    return [pltpu.make_async_copy(w_hbm[m].at[index[m]], stage[m].at[slot], sem.at[m, slot])
            for m in range(3)]


def _swiglu_chunk(hbf, res, j):
    h = hbf[...]
    gate = jnp.dot(h, res[0][j], preferred_element_type=F32)
    up = jnp.dot(h, res[1][j], preferred_element_type=F32)
    a = (gate * _sigmoid(gate) * up).astype(BF16)
    return jnp.dot(a, res[2][j], preferred_element_type=F32)


def _swiglu_tile(load_weights, hbf, w_hbm, lead, res, stage, sem, acc, nj, tf, emit):
    @pl.when(load_weights)
    def _():
        for c in _ffn_chunk_copies(w_hbm, lead, 0, tf, stage, sem, 0):
            c.start()
        for j in range(nj):
            slot = j % 2
            if j + 1 < nj:
                for c in _ffn_chunk_copies(w_hbm, lead, j + 1, tf, stage, sem, 1 - slot):
                    c.start()
            for c in _ffn_chunk_copies(w_hbm, lead, j, tf, stage, sem, slot):
                c.wait()
            for m in range(3):
                res[m][j] = stage[m][slot].astype(BF16)
            yj = _swiglu_chunk(hbf, res, j)
            if j == 0:
                acc[...] = yj
            else:
                acc[...] += yj
        emit(acc[...])

    @pl.when(jnp.logical_not(load_weights))
    def _():
        y = _swiglu_chunk(hbf, res, 0)
        for j in range(1, nj):
            y = y + _swiglu_chunk(hbf, res, j)
        emit(y)


def _ffn_scratch(tm, nj, tf):
    return [
        pltpu.VMEM((nj, D_MODEL, tf), BF16),
        pltpu.VMEM((nj, D_MODEL, tf), BF16),
        pltpu.VMEM((nj, tf, D_MODEL), BF16),
        pltpu.VMEM((2, D_MODEL, tf), F32),
        pltpu.VMEM((2, D_MODEL, tf), F32),
        pltpu.VMEM((2, tf, D_MODEL), F32),
        pltpu.SemaphoreType.DMA((3, 2)),
        pltpu.VMEM((tm, D_MODEL), BF16),
        pltpu.VMEM((tm, D_MODEL), F32),
    ]


def _dense_ffn_kernel(layer, nj, tf, x_ref, g_ref, wg, wu, wd, o_ref,
                      rg, ru, rd, sg, su, sd, sem, hbf, acc):
    hbf[...] = _rms_norm_bf16(x_ref[...], g_ref[...])

    def emit(y):
        o_ref[...] = x_ref[...] + y

    _swiglu_tile(pl.program_id(0) == 0, hbf, (wg, wu, wd), (layer,), (rg, ru, rd),
                 (sg, su, sd), sem, acc, nj, tf, emit)


def _dense_ffn(x, g, wg, wu, wd, layer):
    t = x.shape[0]
    tm = TM_FFN
    tf = TF_DENSE
    nj = wg.shape[2] // tf
    hbm = pl.BlockSpec(memory_space=pl.ANY)
    return pl.pallas_call(
        functools.partial(_dense_ffn_kernel, layer, nj, tf),
        grid=(t // tm,),
        in_specs=[
            pl.BlockSpec((tm, D_MODEL), lambda i: (i, 0)),
            pl.BlockSpec((1, D_MODEL), lambda i: (0, 0)),
            hbm, hbm, hbm,
        ],
        out_specs=pl.BlockSpec((tm, D_MODEL), lambda i: (i, 0)),
        out_shape=jax.ShapeDtypeStruct((t, D_MODEL), F32),
        scratch_shapes=_ffn_scratch(tm, nj, tf),
        compiler_params=_params(("arbitrary",)),
        name="dense_ffn",
    )(x, g, wg, wu, wd)


def _token_rows(ref, row):
    return ref.at[pl.ds(pl.multiple_of(row * ROW_CHUNKS, ROW_CHUNKS), ROW_CHUNKS), :]


def _dispatch_kernel(pos_ref, x_ref, g_ref, xs_in_ref, xs_ref, buf, sem):
    del xs_in_ref
    tm = x_ref.shape[0]
    x = x_ref[...]
    ms = jnp.mean(x * x, axis=-1, keepdims=True)
    h = x * lax.rsqrt(ms + NORM_EPS) * g_ref[...]
    for c in range(ROW_CHUNKS):
        buf[pl.ds(c, tm, stride=ROW_CHUNKS), :] = h[:, c * V7X_LANES:(c + 1) * V7X_LANES]

    def issue(r, carry):
        src = _token_rows(buf, r)
        for k in range(2):
            dst = _token_rows(xs_ref, pos_ref[0, 0, 2 * r + k])
            pltpu.make_async_copy(src, dst, sem).start(priority=k)
        return carry

    lax.fori_loop(0, tm, issue, 0)
    for k in range(2):
        pltpu.make_async_copy(buf, xs_ref.at[pl.ds(0, tm * ROW_CHUNKS), :], sem).wait()


def _dispatch(pos_tiles, x, g, n_rows):
    t = x.shape[0]
    tm = TM_TOK
    zeros = jnp.zeros((n_rows * ROW_CHUNKS, V7X_LANES), F32)
    return pl.pallas_call(
        _dispatch_kernel,
        grid=(t // tm,),
        in_specs=[
            pl.BlockSpec((1, 1, 2 * tm), lambda i: (i, 0, 0), memory_space=pltpu.SMEM),
            pl.BlockSpec((tm, D_MODEL), lambda i: (i, 0)),
            pl.BlockSpec((1, D_MODEL), lambda i: (0, 0)),
            pl.BlockSpec(memory_space=pl.ANY),
        ],
        out_specs=pl.BlockSpec(memory_space=pl.ANY),
        out_shape=jax.ShapeDtypeStruct(zeros.shape, F32),
        scratch_shapes=[pltpu.VMEM((tm * ROW_CHUNKS, V7X_LANES), F32),
                        pltpu.SemaphoreType.DMA(())],
        input_output_aliases={3: 0},
        compiler_params=_params(("arbitrary",)),
        name="moe_dispatch",
    )(pos_tiles, x, g, zeros)


def _moe_ffn_kernel(layer, nj, tf, te_ref, tr_ref, tl_ref, xs_ref, wg, wu, wd, ys_ref,
                    rg, ru, rd, sg, su, sd, sem, hbf, acc):
    i = pl.program_id(0)
    tm = hbf.shape[0]
    rows = tr_ref[i]

    @pl.when(rows > 0)
    def _():
        for c in range(ROW_CHUNKS):
            hbf[:, c * V7X_LANES:(c + 1) * V7X_LANES] = (
                xs_ref[pl.ds(c, tm, stride=ROW_CHUNKS), :].astype(BF16))

        def emit(y):
            for c in range(ROW_CHUNKS):
                ys_ref[pl.ds(c, tm, stride=ROW_CHUNKS), :] = y[:, c * V7X_LANES:(c + 1) * V7X_LANES]

        _swiglu_tile(tl_ref[i] != 0, hbf, (wg, wu, wd), (layer, te_ref[i]), (rg, ru, rd),
                     (sg, su, sd), sem, acc, nj, tf, emit)

    @pl.when(rows == 0)
    def _():
        ys_ref[...] = jnp.zeros_like(ys_ref)


def _moe_ffn(tile_expert, tile_rows, tile_load, xs, wg, wu, wd, layer):
    n_tiles = tile_expert.shape[0]
    tm = TM_MOE
    tf = TF_MOE
    nj = wg.shape[3] // tf
    hbm = pl.BlockSpec(memory_space=pl.ANY)
    grid_spec = pltpu.PrefetchScalarGridSpec(
        num_scalar_prefetch=3,
        grid=(n_tiles,),
        in_specs=[
            pl.BlockSpec((tm * ROW_CHUNKS, V7X_LANES), lambda i, te, tr, tl: (i, 0)),
            hbm, hbm, hbm,
        ],
        out_specs=pl.BlockSpec((tm * ROW_CHUNKS, V7X_LANES), lambda i, te, tr, tl: (i, 0)),
        scratch_shapes=_ffn_scratch(tm, nj, tf),
    )
    return pl.pallas_call(
        functools.partial(_moe_ffn_kernel, layer, nj, tf),
        grid_spec=grid_spec,
        out_shape=jax.ShapeDtypeStruct(xs.shape, F32),
        compiler_params=_params(("arbitrary",)),
        name="moe_ffn",
    )(tile_expert, tile_rows, tile_load, xs, wg, wu, wd)


def _combine_kernel(pos_ref, x_ref, w_ref, ys_ref, o_ref, y1buf, y2buf, sem):
    tm = x_ref.shape[0]
    half = tm // 2

    def issue(h):
        def body(r, carry):
            for k, ybuf in ((0, y1buf), (1, y2buf)):
                src = _token_rows(ys_ref, pos_ref[0, 0, 2 * r + k])
                pltpu.make_async_copy(src, _token_rows(ybuf, r), sem.at[h]).start(priority=k)
            return carry
        lax.fori_loop(h * half, (h + 1) * half, body, 0, unroll=4)

    issue(0)
    issue(1)
    for h in range(2):
        buf_rows = pl.ds(h * half * ROW_CHUNKS, half * ROW_CHUNKS)
        for ybuf in (y1buf, y2buf):
            pltpu.make_async_copy(ys_ref.at[pl.ds(0, half * ROW_CHUNKS), :], ybuf.at[buf_rows, :],
                                  sem.at[h]).wait()
        tok = slice(h * half, (h + 1) * half)
        w1 = w_ref[tok, 0:1]
        w2 = w_ref[tok, 1:2]
        for c in range(ROW_CHUNKS):
            cols = slice(c * V7X_LANES, (c + 1) * V7X_LANES)
            y1 = y1buf[pl.ds(h * half * ROW_CHUNKS + c, half, stride=ROW_CHUNKS), :]
            y2 = y2buf[pl.ds(h * half * ROW_CHUNKS + c, half, stride=ROW_CHUNKS), :]
            o_ref[tok, cols] = x_ref[tok, cols] + (w1 * y1 + w2 * y2)


def _combine(pos_tiles, x, gate_w, ys):
    t = x.shape[0]
    tm = TM_TOK
    return pl.pallas_call(
        _combine_kernel,
        grid=(t // tm,),
        in_specs=[
            pl.BlockSpec((1, 1, 2 * tm), lambda i: (i, 0, 0), memory_space=pltpu.SMEM),
            pl.BlockSpec((tm, D_MODEL), lambda i: (i, 0)),
            pl.BlockSpec((tm, 2), lambda i: (i, 0)),
            pl.BlockSpec(memory_space=pl.ANY),
        ],
        out_specs=pl.BlockSpec((tm, D_MODEL), lambda i: (i, 0)),
        out_shape=jax.ShapeDtypeStruct((t, D_MODEL), F32),
        scratch_shapes=[pltpu.VMEM((tm * ROW_CHUNKS, V7X_LANES), F32),
                        pltpu.VMEM((tm * ROW_CHUNKS, V7X_LANES), F32),
                        pltpu.SemaphoreType.DMA((2,))],
        compiler_params=_params(("arbitrary",)),
        name="moe_combine",
    )(pos_tiles, x, gate_w, ys)


def _moe_layer(x, route, counts, g, wg, wu, wd, layer):
    t = x.shape[0]
    tm = TM_MOE
    n_tiles = (2 * t) // tm + N_EXPERTS
    per_token = lambda rows: rows.transpose(0, 2, 1).reshape(t, 2)
    expert = per_token(route[:, 0:2]).astype(jnp.int32)
    gate_w = per_token(route[:, 2:4])
    rank = per_token(route[:, 4:6]).astype(jnp.int32)
    count = counts[:, 0].astype(jnp.int32)
    padded = ((count + tm - 1) // tm) * tm
    gend = jnp.cumsum(padded)
    gstart = gend - padded
    ids = jnp.arange(N_EXPERTS, dtype=jnp.int32)
    pos = rank + jnp.sum(jnp.where(expert[..., None] == ids, gstart, 0), axis=-1)
    tile_start = jnp.arange(n_tiles, dtype=jnp.int32) * tm
    last_expert = jnp.max(jnp.where(padded > 0, ids, 0))
    tile_expert = jnp.minimum(
        jnp.sum((tile_start[:, None] >= gend[None, :]).astype(jnp.int32), axis=1), last_expert)
    used_end = (gstart + count)[tile_expert]
    tile_rows = jnp.clip(used_end - tile_start, 0, tm)
    prev_expert = jnp.concatenate([jnp.full((1,), -1, jnp.int32), tile_expert[:-1]])
    tile_load = ((tile_rows > 0) & (tile_expert != prev_expert)).astype(jnp.int32)
    pos_tiles = pos.reshape(t // TM_TOK, 1, 2 * TM_TOK)

    xs = _dispatch(pos_tiles, x, g, n_tiles * tm)
    ys = _moe_ffn(tile_expert, tile_rows, tile_load, xs, wg, wu, wd, layer)
    return _combine(pos_tiles, x, gate_w, ys)


def kernel(x, attn_norm_g, w_in, q_norm_g, k_norm_g, sinks, conv_w, conv_b, conv_ln_g,
           conv_ln_b, w_out, ffn_norm_g, dense_w_gate, dense_w_up, dense_w_down,
           w_router, moe_w_gate, moe_w_up, moe_w_down):
    batch, seq, d = x.shape
    depth = w_in.shape[0]
    t = batch * seq
    assert d == D_MODEL and seq % TC_CONV == 0 and t % TM_PROJ == 0 and t % TM_TOK == 0
    assert seq % (ATTN_BLOCKS * BLOCK) == 0
    assert w_in.shape[2] == IN_PROJ_WIDTH and conv_w.shape[1] == CONV_KERNEL
    assert dense_w_gate.shape[2] % TF_DENSE == 0 and moe_w_gate.shape[3] % TF_MOE == 0

    xt = x.reshape(t, d)
    head_id = jnp.arange(SEG_WIDTH) // HEAD_DIM
    seg = (head_id[:, None] == head_id[None, :]).astype(BF16)
    scale = HEAD_DIM ** -0.5 * LOG2_E

    for layer in range(depth):
        gain = jnp.concatenate([jnp.tile(q_norm_g[layer], N_Q_HEADS) * scale,
                                jnp.tile(k_norm_g[layer], N_KV_HEADS)])[None, :]
        q, kd, vd, glu = _mixer_in(xt, attn_norm_g[layer][None, :], w_in, layer, gain, seg)
        attn = _attention(q, kd, vd, _attention_bias(sinks[layer]), batch, seq)
        conv = _conv(glu, _conv_weights(conv_w[layer]), conv_b[layer][None, :],
                     conv_ln_g[layer][None, :],
                     conv_ln_b[layer][None, :], seq)
        ffn_g = ffn_norm_g[layer][None, :]
        i = layer // 2
        if layer % 2 == 0:
            xm = _mixer_out(attn, conv, xt, w_out, layer)
            xt = _dense_ffn(xm, ffn_g, dense_w_gate, dense_w_up, dense_w_down, i)
        else:
            wr = jnp.zeros((D_MODEL, V7X_LANES), F32).at[:, :N_EXPERTS].set(w_router[i])
            wr_hi = wr.astype(BF16)
            wr_lo = (wr - wr_hi.astype(F32)).astype(BF16)
            wr_split = jnp.concatenate(
                [jnp.concatenate([wr_hi, wr_lo], axis=1),
                 jnp.concatenate([wr_hi, jnp.zeros_like(wr_lo)], axis=1)], axis=0)
            xm, route, counts = _mixer_out(attn, conv, xt, w_out, layer,
                                           router=(ffn_g, wr_split))
            xt = _moe_layer(xm, route, counts, ffn_g, moe_w_gate, moe_w_up, moe_w_down, i)
    return xt.reshape(batch, seq, d)
```

```python
import functools

import jax
import jax.numpy as jnp
from jax import lax
from jax.experimental import pallas as pl
from jax.experimental.pallas import tpu as pltpu

F32 = jnp.float32
BF16 = jnp.bfloat16

D_MODEL = 1024
HEAD_DIM = 64
N_Q_HEADS = 8
N_KV_HEADS = 2
GQA_GROUP = N_Q_HEADS // N_KV_HEADS
ATTN_WIDTH = N_Q_HEADS * HEAD_DIM
KV_WIDTH = N_KV_HEADS * HEAD_DIM
QK_WIDTH = ATTN_WIDTH + KV_WIDTH
SEG_WIDTH = 256
BLOCK = 128
CONV_WIDTH = D_MODEL - ATTN_WIDTH
CONV_KERNEL = 31
IN_PROJ_WIDTH = ATTN_WIDTH + 2 * KV_WIDTH + 2 * CONV_WIDTH
N_EXPERTS = 8
NORM_EPS = 1e-6
LOG2_E = 1.4426950408889634

V7X_LANES = 128
V7X_SUBLANES = 8
V7X_VMEM_LIMIT_BYTES = 56 * 1024 * 1024
ROW_CHUNKS = D_MODEL // V7X_LANES

TM_PROJ = 512
ATTN_BLOCKS = 4
TC_CONV = 256
CONV_HALO = 32
CONV_SLABS = CONV_WIDTH // V7X_LANES
CONV_SEGS = 4
CONV_SEG_ROWS = TC_CONV // CONV_SEGS
CONV_SEG_LEN = CONV_SEG_ROWS + CONV_HALO
TM_FFN = 512
TF_DENSE = 256
TM_MOE = 512
TF_MOE = 512
TM_TOK = 512


def _params(sem, vmem=V7X_VMEM_LIMIT_BYTES):
    return pltpu.CompilerParams(dimension_semantics=sem, vmem_limit_bytes=vmem)


def _sigmoid(x):
    return 1.0 / (1.0 + jnp.exp(-x))


def _rms_norm_bf16(x, g):
    ms = jnp.mean(x * x, axis=-1, keepdims=True)
    return (x * lax.rsqrt(ms + NORM_EPS) * g).astype(BF16)


def _mixer_in_kernel(x_ref, g_ref, w_ref, gain_ref, seg_ref, q_ref, k_ref, v_ref, glu_ref, w_bf16):
    @pl.when(pl.program_id(0) == 0)
    def _():
        w_bf16[...] = w_ref[...].astype(BF16)

    h = _rms_norm_bf16(x_ref[...], g_ref[...])
    proj = jnp.dot(h, w_bf16[...], preferred_element_type=F32)
    qk = proj[:, :QK_WIDTH]
    sq = qk * qk
    hi = sq.astype(BF16)
    lo = (sq - hi.astype(F32)).astype(BF16)
    seg = seg_ref[...]
    parts = []
    for c0 in range(0, QK_WIDTH, SEG_WIDTH):
        width = min(SEG_WIDTH, QK_WIDTH - c0)
        blk = slice(c0, c0 + width)
        parts.append(jnp.dot(hi[:, blk], seg[:width, :width], preferred_element_type=F32)
                     + jnp.dot(lo[:, blk], seg[:width, :width], preferred_element_type=F32))
    ss = jnp.concatenate(parts, axis=1)
    qkn = qk * lax.rsqrt(ss * (1.0 / HEAD_DIM) + NORM_EPS) * gain_ref[...]
    q_ref[...] = qkn[:, :ATTN_WIDTH].astype(BF16)
    lane = lax.broadcasted_iota(jnp.int32, (x_ref.shape[0], KV_WIDTH), 1)
    low = lane < HEAD_DIM
    for src, dst in ((qkn[:, ATTN_WIDTH:QK_WIDTH], k_ref),
                     (proj[:, QK_WIDTH:QK_WIDTH + KV_WIDTH], v_ref)):
        rot = pltpu.roll(src, HEAD_DIM, axis=1)
        dst[:, :KV_WIDTH] = jnp.where(low, src, rot).astype(BF16)
        dst[:, KV_WIDTH:] = jnp.where(low, rot, src).astype(BF16)
    o_c = QK_WIDTH + KV_WIDTH
    u = proj[:, o_c:o_c + CONV_WIDTH]
    gate = proj[:, o_c + CONV_WIDTH:]
    glu_ref[...] = u * _sigmoid(gate)


def _mixer_in(x, g, w_in, layer, gain, seg):
    t = x.shape[0]
    tm = TM_PROJ
    row = lambda i: (i, 0)
    fixed = lambda i: (0, 0)
    return pl.pallas_call(
        _mixer_in_kernel,
        grid=(t // tm,),
        in_specs=[
            pl.BlockSpec((tm, D_MODEL), row),
            pl.BlockSpec((1, D_MODEL), fixed),
            pl.BlockSpec((None, D_MODEL, IN_PROJ_WIDTH), lambda i: (layer, 0, 0)),
            pl.BlockSpec((1, QK_WIDTH), fixed),
            pl.BlockSpec((SEG_WIDTH, SEG_WIDTH), fixed),
        ],
        out_specs=[
            pl.BlockSpec((tm, ATTN_WIDTH), row),
            pl.BlockSpec((tm, 2 * KV_WIDTH), row),
            pl.BlockSpec((tm, 2 * KV_WIDTH), row),
            pl.BlockSpec((tm, CONV_WIDTH), row),
        ],
        out_shape=[
            jax.ShapeDtypeStruct((t, ATTN_WIDTH), BF16),
            jax.ShapeDtypeStruct((t, 2 * KV_WIDTH), BF16),
            jax.ShapeDtypeStruct((t, 2 * KV_WIDTH), BF16),
            jax.ShapeDtypeStruct((t, CONV_WIDTH), F32),
        ],
        scratch_shapes=[pltpu.VMEM((D_MODEL, IN_PROJ_WIDTH), BF16)],
        compiler_params=_params(("arbitrary",)),
        name="mixer_in",
    )(x, g, w_in, gain, seg)


def _attn_kernel(q_ref, kc_ref, kp_ref, vc_ref, vp_ref, bias0_ref, bias_ref, o_ref):
    n_keys = (ATTN_BLOCKS + 1) * BLOCK
    low = lax.broadcasted_iota(jnp.int32, (n_keys, 2 * HEAD_DIM), 1) < HEAD_DIM
    low_q = lax.broadcasted_iota(jnp.int32, (BLOCK, 2 * HEAD_DIM), 1) < HEAD_DIM
    key0 = lax.broadcasted_iota(jnp.int32, (2 * BLOCK, 2 * HEAD_DIM), 0) == 0
    zero = jnp.zeros((), BF16)
    contract_last = (((1,), (1,)), ((), ()))
    for g in range(N_KV_HEADS):
        cols = slice(g * 2 * HEAD_DIM, (g + 1) * 2 * HEAD_DIM)
        k_all = jnp.concatenate([kp_ref[:, cols], kc_ref[:, cols]], axis=0)
        v_all = jnp.concatenate([vp_ref[:, cols], vc_ref[:, cols]], axis=0)
        k_lo_all = jnp.where(low, k_all, zero)
        k_hi_all = jnp.where(low, zero, k_all)
        pair0 = slice((2 * g) * 2 * HEAD_DIM, (2 * g + 1) * 2 * HEAD_DIM)
        pair1 = slice((2 * g + 1) * 2 * HEAD_DIM, (2 * g + 2) * 2 * HEAD_DIM)
        for blk in range(ATTN_BLOCKS):
            rows = slice(blk * BLOCK, (blk + 1) * BLOCK)
            band = slice(blk * BLOCK, (blk + 2) * BLOCK)
            b_ref = bias0_ref if blk == 0 else bias_ref
            k_lo = jnp.where(key0, zero, k_lo_all[band])
            k_hi = jnp.where(key0, zero, k_hi_all[band])
            v_band = jnp.where(key0, zero, v_all[band])
            v_ext = jnp.concatenate([v_band, jnp.ones_like(v_band)], axis=1)
            q2 = jnp.concatenate([q_ref[rows, pair0], q_ref[rows, pair1]], axis=0)
            s_lo = lax.dot_general(q2, k_lo, contract_last, preferred_element_type=F32)
            s_hi = lax.dot_general(q2, k_hi, contract_last, preferred_element_type=F32)
            sh = jnp.concatenate([s_lo, s_hi], axis=0) + b_ref[0, g]
            m = jnp.max(sh, axis=-1, keepdims=True)
            p = jnp.exp2(sh - m).astype(BF16)
            pv = jnp.dot(p, v_ext, preferred_element_type=F32)
            out = pv[:, :2 * HEAD_DIM] / pv[:, 2 * HEAD_DIM:]
            o_ref[rows, pair0] = jnp.where(low_q, out[:BLOCK], out[2 * BLOCK:3 * BLOCK]).astype(BF16)
            o_ref[rows, pair1] = jnp.where(low_q, out[BLOCK:2 * BLOCK], out[3 * BLOCK:]).astype(BF16)


def _attention(q, kd, vd, bias, batch, seq):
    t = q.shape[0]
    tq = ATTN_BLOCKS * BLOCK
    nt = seq // tq
    cur = lambda b, n: (b * nt + n, 0)
    prev = lambda b, n: ((b * nt + n) * ATTN_BLOCKS - jnp.minimum(n, 1), 0)
    bias_spec = lambda index_map: pl.BlockSpec(
        (1, N_KV_HEADS, GQA_GROUP * BLOCK, 2 * BLOCK), index_map)
    return pl.pallas_call(
        _attn_kernel,
        grid=(batch, nt),
        in_specs=[
            pl.BlockSpec((tq, ATTN_WIDTH), cur),
            pl.BlockSpec((tq, 2 * KV_WIDTH), cur),
            pl.BlockSpec((BLOCK, 2 * KV_WIDTH), prev),
            pl.BlockSpec((tq, 2 * KV_WIDTH), cur),
            pl.BlockSpec((BLOCK, 2 * KV_WIDTH), prev),
            bias_spec(lambda b, n: (jnp.minimum(n, 1), 0, 0, 0)),
            bias_spec(lambda b, n: (1, 0, 0, 0)),
        ],
        out_specs=pl.BlockSpec((tq, ATTN_WIDTH), cur),
        out_shape=jax.ShapeDtypeStruct((t, ATTN_WIDTH), BF16),
        compiler_params=_params(("arbitrary", "arbitrary")),
        name="attention",
    )(q, kd, kd, vd, vd, bias, bias)


def _attention_bias(sinks):
    heads = jnp.arange(1, N_Q_HEADS + 1, dtype=F32)
    slopes = jnp.exp2(-8.0 * heads / N_Q_HEADS)
    key = jnp.arange(2 * BLOCK)[None, :]
    dist = (jnp.arange(BLOCK)[:, None] + BLOCK) - key
    valid = (dist >= 0) & (dist < BLOCK)
    valid = jnp.stack([valid & (key >= BLOCK), valid])
    bias = -slopes[:, None, None] * dist.astype(F32)
    bias = jnp.where(valid[:, None], bias[None], -jnp.inf)
    bias = jnp.where(key == 0, sinks[None, :, None, None], bias) * LOG2_E
    order = jnp.array([[4 * g, 4 * g + 2, 4 * g + 1, 4 * g + 3] for g in range(N_KV_HEADS)])
    return bias[:, order].reshape(2, N_KV_HEADS, GQA_GROUP * BLOCK, 2 * BLOCK)


def _conv_kernel(tiles_per_seq, cur_ref, halo_ref, w_ref, b_ref, lng_ref, lnb_ref, o_ref, buf, ybuf):
    first = (pl.program_id(0) % tiles_per_seq) == 0
    groups = CONV_SEG_LEN // V7X_SUBLANES
    for s in range(CONV_SLABS):
        cols = slice(s * V7X_LANES, (s + 1) * V7X_LANES)
        for j in range(CONV_SEGS):
            for gi in range(groups):
                t0 = CONV_SEG_ROWS * j - CONV_HALO + V7X_SUBLANES * gi
                if t0 < 0:
                    h0 = CONV_HALO + t0
                    src = jnp.where(first, 0.0, halo_ref[h0:h0 + V7X_SUBLANES, cols])
                else:
                    src = cur_ref[t0:t0 + V7X_SUBLANES, cols]
                buf[s, pl.ds(CONV_SEGS * V7X_SUBLANES * gi + j, V7X_SUBLANES, stride=CONV_SEGS), :] = src

    base = CONV_HALO - (CONV_KERNEL - 1)
    out_groups = CONV_SEG_ROWS // V7X_SUBLANES

    def seg_body(it, carry):
        s = it // CONV_SEGS
        j = it % CONV_SEGS
        acc = [None] * out_groups
        for k in range(CONV_KERNEL):
            wk = w_ref[s, k]
            for gi in range(out_groups):
                i = V7X_SUBLANES * gi + base + k
                term = buf[s, pl.ds(CONV_SEGS * i + j, V7X_SUBLANES, stride=CONV_SEGS), :] * wk
                acc[gi] = term if acc[gi] is None else acc[gi] + term
        row0 = pl.multiple_of(j * CONV_SEG_ROWS, CONV_SEG_ROWS)
        ybuf[s, pl.ds(row0, CONV_SEG_ROWS), :] = jnp.concatenate(acc, axis=0)
        return carry

    lax.fori_loop(0, CONV_SLABS * CONV_SEGS, seg_body, 0, unroll=2)

    y = jnp.concatenate([ybuf[s] for s in range(CONV_SLABS)], axis=1) + b_ref[...]
    mu = jnp.mean(y, axis=-1, keepdims=True)
    yc = y - mu
    var = jnp.mean(yc * yc, axis=-1, keepdims=True)
    z = yc * lax.rsqrt(var + NORM_EPS) * lng_ref[...] + lnb_ref[...]
    o_ref[...] = (z * _sigmoid(z)).astype(BF16)


def _conv_weights(w):
    ws = w.reshape(CONV_KERNEL, CONV_SLABS, V7X_LANES).transpose(1, 0, 2)
    return jnp.broadcast_to(ws[:, :, None, :], (CONV_SLABS, CONV_KERNEL, V7X_SUBLANES, V7X_LANES))


def _conv(glu, w_slab, b, lng, lnb, seq):
    t = glu.shape[0]
    tc = TC_CONV
    ratio = tc // CONV_HALO
    fixed = lambda i: (0, 0)
    return pl.pallas_call(
        functools.partial(_conv_kernel, seq // tc),
        grid=(t // tc,),
        in_specs=[
            pl.BlockSpec((tc, CONV_WIDTH), lambda i: (i, 0)),
            pl.BlockSpec((CONV_HALO, CONV_WIDTH), lambda i: (jnp.maximum(i * ratio - 1, 0), 0)),
            pl.BlockSpec((CONV_SLABS, CONV_KERNEL, V7X_SUBLANES, V7X_LANES), lambda i: (0, 0, 0, 0)),
            pl.BlockSpec((1, CONV_WIDTH), fixed),
            pl.BlockSpec((1, CONV_WIDTH), fixed),
            pl.BlockSpec((1, CONV_WIDTH), fixed),
        ],
        out_specs=pl.BlockSpec((tc, CONV_WIDTH), lambda i: (i, 0)),
        out_shape=jax.ShapeDtypeStruct((t, CONV_WIDTH), BF16),
        scratch_shapes=[pltpu.VMEM((CONV_SLABS, CONV_SEGS * CONV_SEG_LEN, V7X_LANES), F32),
                        pltpu.VMEM((CONV_SLABS, tc, V7X_LANES), F32)],
        compiler_params=_params(("arbitrary",)),
        name="conv",
    )(glu, glu, w_slab, b, lng, lnb)


def _out_proj_residual(attn_ref, conv_ref, x_ref, w_ref, w_bf16):
    @pl.when(pl.program_id(0) == 0)
    def _():
        w_bf16[...] = w_ref[...].astype(BF16)

    y = (jnp.dot(attn_ref[...], w_bf16[:ATTN_WIDTH], preferred_element_type=F32)
         + jnp.dot(conv_ref[...], w_bf16[ATTN_WIDTH:], preferred_element_type=F32))
    return x_ref[...] + y


def _mixer_out_router_kernel(attn_ref, conv_ref, x_ref, w_ref, g_ref, wr_ref,
                             o_ref, route_ref, count_ref, w_bf16, tri, carry):
    i = pl.program_id(0)
    tm = x_ref.shape[0]

    @pl.when(i == 0)
    def _():
        carry[...] = jnp.zeros_like(carry)
        earlier = (lax.broadcasted_iota(jnp.int32, (tm, tm), 0)
                   < lax.broadcasted_iota(jnp.int32, (tm, tm), 1))
        tri[...] = jnp.where(earlier, 1.0, 0.0).astype(BF16)

    x = _out_proj_residual(attn_ref, conv_ref, x_ref, w_ref, w_bf16)
    o_ref[...] = x

    ms = jnp.mean(x * x, axis=-1, keepdims=True)
    h = x * lax.rsqrt(ms + NORM_EPS) * g_ref[...]
    h_hi = h.astype(BF16)
    h_lo = (h - h_hi.astype(F32)).astype(BF16)
    both = jnp.dot(jnp.concatenate([h_hi, h_lo], axis=1), wr_ref[...],
                   preferred_element_type=F32)
    logits = both[:, :V7X_LANES] + both[:, V7X_LANES:]

    lt = jnp.transpose(logits)[:N_EXPERTS]
    sub = lax.broadcasted_iota(jnp.int32, (N_EXPERTS, tm), 0)
    sub_f = sub.astype(F32)
    neg_inf = -jnp.inf
    none = float(N_EXPERTS)
    m1 = jnp.max(lt, axis=0, keepdims=True)
    i1 = jnp.min(jnp.where(lt == m1, sub_f, none), axis=0, keepdims=True)
    oh1 = sub_f == i1
    lt2 = jnp.where(oh1, neg_inf, lt)
    m2 = jnp.max(lt2, axis=0, keepdims=True)
    i2 = jnp.min(jnp.where(lt2 == m2, sub_f, none), axis=0, keepdims=True)
    oh2 = sub_f == i2
    e = jnp.exp(m2 - m1)
    w1 = 1.0 / (1.0 + e)
    w2 = e / (1.0 + e)

    member = jnp.where(oh1 | oh2, 1.0, 0.0)
    before = (jnp.dot(member.astype(BF16), tri[...], preferred_element_type=F32)
              + carry[:, 0:1])
    rank1 = jnp.sum(jnp.where(oh1, before, 0.0), axis=0, keepdims=True)
    rank2 = jnp.sum(jnp.where(oh2, before, 0.0), axis=0, keepdims=True)
    carry[...] = carry[...] + jnp.sum(member, axis=1, keepdims=True)

    route = jnp.where(sub == 0, i1, 0.0)
    route = jnp.where(sub == 1, i2, route)
    route = jnp.where(sub == 2, w1, route)
    route = jnp.where(sub == 3, w2, route)
    route = jnp.where(sub == 4, rank1, route)
    route = jnp.where(sub == 5, rank2, route)
    route_ref[0] = route
    count_ref[...] = carry[...]


def _mixer_out_router(attn, conv, x, w_out, layer, g, wr_split):
    t = x.shape[0]
    tm = TM_PROJ
    row = lambda i: (i, 0)
    fixed = lambda i: (0, 0)
    in_specs = [
        pl.BlockSpec((tm, ATTN_WIDTH), row),
        pl.BlockSpec((tm, CONV_WIDTH), row),
        pl.BlockSpec((tm, D_MODEL), row),
        pl.BlockSpec((None, D_MODEL, D_MODEL), lambda i: (layer, 0, 0)),
    ]
    w_scratch = pltpu.VMEM((D_MODEL, D_MODEL), BF16)
    return pl.pallas_call(
        _mixer_out_router_kernel,
        grid=(t // tm,),
        in_specs=in_specs + [
            pl.BlockSpec((1, D_MODEL), fixed),
            pl.BlockSpec((2 * D_MODEL, 2 * V7X_LANES), fixed),
        ],
        out_specs=[
            pl.BlockSpec((tm, D_MODEL), row),
            pl.BlockSpec((1, N_EXPERTS, tm), lambda i: (i, 0, 0)),
            pl.BlockSpec((N_EXPERTS, V7X_LANES), fixed),
        ],
        out_shape=[
            jax.ShapeDtypeStruct((t, D_MODEL), F32),
            jax.ShapeDtypeStruct((t // tm, N_EXPERTS, tm), F32),
            jax.ShapeDtypeStruct((N_EXPERTS, V7X_LANES), F32),
        ],
        scratch_shapes=[w_scratch, pltpu.VMEM((tm, tm), BF16),
                        pltpu.VMEM((N_EXPERTS, V7X_LANES), F32)],
        compiler_params=_params(("arbitrary",)),
        name="mixer_out_router",
    )(attn, conv, x, w_out, g, wr_split)


def _ffn_chunk_copies(w_hbm, lead, j, tf, stage, sem, slot):
    cols = pl.ds(j * tf, tf)
    every = slice(None)
    index = (lead + (every, cols), lead + (every, cols), lead + (cols, every))
    return [pltpu.make_async_copy(w_hbm[m].at[index[m]], stage[m].at[slot], sem.at[m, slot])
            for m in range(3)]


def _swiglu_chunk(hbf, res, j):
    h = hbf[...]
    gate = jnp.dot(h, res[0][j], preferred_element_type=F32)
    up = jnp.dot(h, res[1][j], preferred_element_type=F32)
    a = (gate * _sigmoid(gate) * up).astype(BF16)
    return jnp.dot(a, res[2][j], preferred_element_type=F32)


def _swiglu_tile(load_weights, hbf, w_hbm, lead, res, stage, sem, acc, nj, tf, emit):
    @pl.when(load_weights)
    def _():
        for c in _ffn_chunk_copies(w_hbm, lead, 0, tf, stage, sem, 0):
            c.start()
        for j in range(nj):
            slot = j % 2
            if j + 1 < nj:
                for c in _ffn_chunk_copies(w_hbm, lead, j + 1, tf, stage, sem, 1 - slot):
                    c.start()
            for c in _ffn_chunk_copies(w_hbm, lead, j, tf, stage, sem, slot):
                c.wait()
            for m in range(3):
                res[m][j] = stage[m][slot].astype(BF16)
            yj = _swiglu_chunk(hbf, res, j)
            if j == 0:
                acc[...] = yj
            else:
                acc[...] += yj
        emit(acc[...])

    @pl.when(jnp.logical_not(load_weights))
    def _():
        y = _swiglu_chunk(hbf, res, 0)
        for j in range(1, nj):
            y = y + _swiglu_chunk(hbf, res, j)
        emit(y)


def _ffn_scratch(tm, nj, tf):
    return [
        pltpu.VMEM((nj, D_MODEL, tf), BF16),
        pltpu.VMEM((nj, D_MODEL, tf), BF16),
        pltpu.VMEM((nj, tf, D_MODEL), BF16),
        pltpu.VMEM((2, D_MODEL, tf), F32),
        pltpu.VMEM((2, D_MODEL, tf), F32),
        pltpu.VMEM((2, tf, D_MODEL), F32),
        pltpu.SemaphoreType.DMA((3, 2)),
        pltpu.VMEM((tm, D_MODEL), BF16),
        pltpu.VMEM((tm, D_MODEL), F32),
    ]


def _dense_ffn_kernel(ffn_layer, nj, tf, attn_ref, conv_ref, x_ref, w_out_ref, g_ref, wg, wu, wd,
                      o_ref, w_out_bf16, x_mid, rg, ru, rd, sg, su, sd, sem, hbf, acc):
    x_mid[...] = _out_proj_residual(attn_ref, conv_ref, x_ref, w_out_ref, w_out_bf16)
    hbf[...] = _rms_norm_bf16(x_mid[...], g_ref[...])

    def emit(y):
        o_ref[...] = x_mid[...] + y

    _swiglu_tile(pl.program_id(0) == 0, hbf, (wg, wu, wd), (ffn_layer,), (rg, ru, rd),
                 (sg, su, sd), sem, acc, nj, tf, emit)


def _mixer_out_dense_ffn(attn, conv, x, w_out, layer, g, wg, wu, wd, ffn_layer):
    t = x.shape[0]
    tm = TM_FFN
    tf = TF_DENSE
    nj = wg.shape[2] // tf
    row = lambda i: (i, 0)
    hbm = pl.BlockSpec(memory_space=pl.ANY)
    return pl.pallas_call(
        functools.partial(_dense_ffn_kernel, ffn_layer, nj, tf),
        grid=(t // tm,),
        in_specs=[
            pl.BlockSpec((tm, ATTN_WIDTH), row),
            pl.BlockSpec((tm, CONV_WIDTH), row),
            pl.BlockSpec((tm, D_MODEL), row),
            pl.BlockSpec((None, D_MODEL, D_MODEL), lambda i: (layer, 0, 0)),
            pl.BlockSpec((1, D_MODEL), lambda i: (0, 0)),
            hbm, hbm, hbm,
        ],
        out_specs=pl.BlockSpec((tm, D_MODEL), row),
        out_shape=jax.ShapeDtypeStruct((t, D_MODEL), F32),
        scratch_shapes=[pltpu.VMEM((D_MODEL, D_MODEL), BF16), pltpu.VMEM((tm, D_MODEL), F32)]
        + _ffn_scratch(tm, nj, tf),
        compiler_params=_params(("arbitrary",)),
        name="dense_ffn",
    )(attn, conv, x, w_out, g, wg, wu, wd)


def _token_rows(ref, row):
    return ref.at[pl.ds(pl.multiple_of(row * ROW_CHUNKS, ROW_CHUNKS), ROW_CHUNKS), :]


def _dispatch_kernel(fill_ref, pos_ref, x_ref, g_ref, xs_ref, buf, sem, fill_sem):
    tm = x_ref.shape[0]

    @pl.when(pl.program_id(0) == 0)
    def _():
        buf[...] = jnp.zeros_like(buf)
        tile_len = TM_MOE * ROW_CHUNKS

        def fill(n):
            start = pl.multiple_of(fill_ref[n + 1] * tile_len, tile_len)
            return pltpu.make_async_copy(buf.at[pl.ds(0, tile_len), :],
                                         xs_ref.at[pl.ds(start, tile_len), :], fill_sem)

        def start_fill(n, carry):
            fill(n).start()
            return carry

        def wait_fill(n, carry):
            fill(n).wait()
            return carry

        lax.fori_loop(0, fill_ref[0], start_fill, 0)
        lax.fori_loop(0, fill_ref[0], wait_fill, 0)

    x = x_ref[...]
    ms = jnp.mean(x * x, axis=-1, keepdims=True)
    h = x * lax.rsqrt(ms + NORM_EPS) * g_ref[...]
    for c in range(ROW_CHUNKS):
        buf[pl.ds(c, tm, stride=ROW_CHUNKS), :] = h[:, c * V7X_LANES:(c + 1) * V7X_LANES]

    def issue(r, carry):
        src = _token_rows(buf, r)
        for k in range(2):
            dst = _token_rows(xs_ref, pos_ref[0, 0, 2 * r + k])
            pltpu.make_async_copy(src, dst, sem).start(priority=k)
        return carry

    lax.fori_loop(0, tm, issue, 0)
    for k in range(2):
        pltpu.make_async_copy(buf, xs_ref.at[pl.ds(0, tm * ROW_CHUNKS), :], sem).wait()


def _dispatch(fill_tiles, pos_tiles, x, g, n_rows):
    t = x.shape[0]
    tm = TM_TOK
    assert tm >= TM_MOE
    grid_spec = pltpu.PrefetchScalarGridSpec(
        num_scalar_prefetch=1,
        grid=(t // tm,),
        in_specs=[
            pl.BlockSpec((1, 1, 2 * tm), lambda i, fill: (i, 0, 0), memory_space=pltpu.SMEM),
            pl.BlockSpec((tm, D_MODEL), lambda i, fill: (i, 0)),
            pl.BlockSpec((1, D_MODEL), lambda i, fill: (0, 0)),
        ],
        out_specs=pl.BlockSpec(memory_space=pl.ANY),
        scratch_shapes=[pltpu.VMEM((tm * ROW_CHUNKS, V7X_LANES), F32),
                        pltpu.SemaphoreType.DMA(()),
                        pltpu.SemaphoreType.DMA(())],
    )
    return pl.pallas_call(
        _dispatch_kernel,
        grid_spec=grid_spec,
        out_shape=jax.ShapeDtypeStruct((n_rows * ROW_CHUNKS, V7X_LANES), F32),
        compiler_params=_params(("arbitrary",)),
        name="moe_dispatch",
    )(fill_tiles, pos_tiles, x, g)


def _moe_ffn_kernel(layer, nj, tf, te_ref, tr_ref, tl_ref, xs_ref, wg, wu, wd, ys_ref,
                    rg, ru, rd, sg, su, sd, sem, hbf, acc):
    i = pl.program_id(0)
    tm = hbf.shape[0]
    rows = tr_ref[i]

    @pl.when(rows > 0)
    def _():
        for c in range(ROW_CHUNKS):
            hbf[:, c * V7X_LANES:(c + 1) * V7X_LANES] = (
                xs_ref[pl.ds(c, tm, stride=ROW_CHUNKS), :].astype(BF16))

        def emit(y):
            for c in range(ROW_CHUNKS):
                ys_ref[pl.ds(c, tm, stride=ROW_CHUNKS), :] = y[:, c * V7X_LANES:(c + 1) * V7X_LANES]

        _swiglu_tile(tl_ref[i] != 0, hbf, (wg, wu, wd), (layer, te_ref[i]), (rg, ru, rd),
                     (sg, su, sd), sem, acc, nj, tf, emit)

    @pl.when(rows == 0)
    def _():
        ys_ref[...] = jnp.zeros_like(ys_ref)


def _moe_ffn(tile_expert, tile_rows, tile_load, xs, wg, wu, wd, layer):
    n_tiles = tile_expert.shape[0]
    tm = TM_MOE
    tf = TF_MOE
    nj = wg.shape[3] // tf
    hbm = pl.BlockSpec(memory_space=pl.ANY)
    grid_spec = pltpu.PrefetchScalarGridSpec(
        num_scalar_prefetch=3,
        grid=(n_tiles,),
        in_specs=[
            pl.BlockSpec((tm * ROW_CHUNKS, V7X_LANES), lambda i, te, tr, tl: (i, 0)),
            hbm, hbm, hbm,
        ],
        out_specs=pl.BlockSpec((tm * ROW_CHUNKS, V7X_LANES), lambda i, te, tr, tl: (i, 0)),
        scratch_shapes=_ffn_scratch(tm, nj, tf),
    )
    return pl.pallas_call(
        functools.partial(_moe_ffn_kernel, layer, nj, tf),
        grid_spec=grid_spec,
        out_shape=jax.ShapeDtypeStruct(xs.shape, F32),
        compiler_params=_params(("arbitrary",)),
        name="moe_ffn",
    )(tile_expert, tile_rows, tile_load, xs, wg, wu, wd)


def _combine_kernel(pos_ref, x_ref, w_ref, ys_ref, o_ref, y1buf, y2buf, sem):
    tm = x_ref.shape[0]
    half = tm // 2

    def issue(h):
        def body(r, carry):
            for k, ybuf in ((0, y1buf), (1, y2buf)):
                src = _token_rows(ys_ref, pos_ref[0, 0, 2 * r + k])
                pltpu.make_async_copy(src, _token_rows(ybuf, r), sem.at[h]).start(priority=k)
            return carry
        lax.fori_loop(h * half, (h + 1) * half, body, 0, unroll=4)

    issue(0)
    issue(1)
    for h in range(2):
        buf_rows = pl.ds(h * half * ROW_CHUNKS, half * ROW_CHUNKS)
        for ybuf in (y1buf, y2buf):
            pltpu.make_async_copy(ys_ref.at[pl.ds(0, half * ROW_CHUNKS), :], ybuf.at[buf_rows, :],
                                  sem.at[h]).wait()
        tok = slice(h * half, (h + 1) * half)
        w1 = w_ref[tok, 0:1]
        w2 = w_ref[tok, 1:2]
        for c in range(ROW_CHUNKS):
            cols = slice(c * V7X_LANES, (c + 1) * V7X_LANES)
            y1 = y1buf[pl.ds(h * half * ROW_CHUNKS + c, half, stride=ROW_CHUNKS), :]
            y2 = y2buf[pl.ds(h * half * ROW_CHUNKS + c, half, stride=ROW_CHUNKS), :]
            o_ref[tok, cols] = x_ref[tok, cols] + (w1 * y1 + w2 * y2)


def _combine(pos_tiles, x, gate_w, ys):
    t = x.shape[0]
    tm = TM_TOK
    return pl.pallas_call(
        _combine_kernel,
        grid=(t // tm,),
        in_specs=[
            pl.BlockSpec((1, 1, 2 * tm), lambda i: (i, 0, 0), memory_space=pltpu.SMEM),
            pl.BlockSpec((tm, D_MODEL), lambda i: (i, 0)),
            pl.BlockSpec((tm, 2), lambda i: (i, 0)),
            pl.BlockSpec(memory_space=pl.ANY),
        ],
        out_specs=pl.BlockSpec((tm, D_MODEL), lambda i: (i, 0)),
        out_shape=jax.ShapeDtypeStruct((t, D_MODEL), F32),
        scratch_shapes=[pltpu.VMEM((tm * ROW_CHUNKS, V7X_LANES), F32),
                        pltpu.VMEM((tm * ROW_CHUNKS, V7X_LANES), F32),
                        pltpu.SemaphoreType.DMA((2,))],
        compiler_params=_params(("arbitrary",)),
        name="moe_combine",
    )(pos_tiles, x, gate_w, ys)


def _moe_layer(x, route, counts, g, wg, wu, wd, layer):
    t = x.shape[0]
    tm = TM_MOE
    n_tiles = (2 * t) // tm + N_EXPERTS
    per_token = lambda rows: rows.transpose(0, 2, 1).reshape(t, 2)
    expert = per_token(route[:, 0:2]).astype(jnp.int32)
    gate_w = per_token(route[:, 2:4])
    rank = per_token(route[:, 4:6]).astype(jnp.int32)
    count = counts[:, 0].astype(jnp.int32)
    padded = ((count + tm - 1) // tm) * tm
    gend = jnp.cumsum(padded)
    gstart = gend - padded
    ids = jnp.arange(N_EXPERTS, dtype=jnp.int32)
    pos = rank + jnp.sum(jnp.where(expert[..., None] == ids, gstart, 0), axis=-1)
    tile_start = jnp.arange(n_tiles, dtype=jnp.int32) * tm
    last_expert = jnp.max(jnp.where(padded > 0, ids, 0))
    tile_expert = jnp.minimum(
        jnp.sum((tile_start[:, None] >= gend[None, :]).astype(jnp.int32), axis=1), last_expert)
    used_end = (gstart + count)[tile_expert]
    tile_rows = jnp.clip(used_end - tile_start, 0, tm)
    prev_expert = jnp.concatenate([jnp.full((1,), -1, jnp.int32), tile_expert[:-1]])
    tile_load = ((tile_rows > 0) & (tile_expert != prev_expert)).astype(jnp.int32)
    pos_tiles = pos.reshape(t // TM_TOK, 1, 2 * TM_TOK)

    max_fill = n_tiles - (2 * t - N_EXPERTS * (tm - 1)) // tm
    partial = tile_rows < tm
    fill_idx = jnp.nonzero(partial, size=max_fill, fill_value=0)[0].astype(jnp.int32)
    fill_tiles = jnp.concatenate([jnp.sum(partial, dtype=jnp.int32)[None], fill_idx])

    xs = _dispatch(fill_tiles, pos_tiles, x, g, n_tiles * tm)
    ys = _moe_ffn(tile_expert, tile_rows, tile_load, xs, wg, wu, wd, layer)
    return _combine(pos_tiles, x, gate_w, ys)


def kernel(x, attn_norm_g, w_in, q_norm_g, k_norm_g, sinks, conv_w, conv_b, conv_ln_g,
           conv_ln_b, w_out, ffn_norm_g, dense_w_gate, dense_w_up, dense_w_down,
           w_router, moe_w_gate, moe_w_up, moe_w_down):
    batch, seq, d = x.shape
    depth = w_in.shape[0]
    t = batch * seq
    assert d == D_MODEL and seq % TC_CONV == 0 and t % TM_PROJ == 0 and t % TM_TOK == 0
    assert seq % (ATTN_BLOCKS * BLOCK) == 0
    assert w_in.shape[2] == IN_PROJ_WIDTH and conv_w.shape[1] == CONV_KERNEL
    assert dense_w_gate.shape[2] % TF_DENSE == 0 and moe_w_gate.shape[3] % TF_MOE == 0

    xt = x.reshape(t, d)
    head_id = jnp.arange(SEG_WIDTH) // HEAD_DIM
    seg = (head_id[:, None] == head_id[None, :]).astype(BF16)
    scale = HEAD_DIM ** -0.5 * LOG2_E

    for layer in range(depth):
        gain = jnp.concatenate([jnp.tile(q_norm_g[layer], N_Q_HEADS) * scale,
                                jnp.tile(k_norm_g[layer], N_KV_HEADS)])[None, :]
        q, kd, vd, glu = _mixer_in(xt, attn_norm_g[layer][None, :], w_in, layer, gain, seg)
        attn = _attention(q, kd, vd, _attention_bias(sinks[layer]), batch, seq)
        conv = _conv(glu, _conv_weights(conv_w[layer]), conv_b[layer][None, :],
                     conv_ln_g[layer][None, :],
                     conv_ln_b[layer][None, :], seq)
        ffn_g = ffn_norm_g[layer][None, :]
        i = layer // 2
        if layer % 2 == 0:
            xt = _mixer_out_dense_ffn(attn, conv, xt, w_out, layer, ffn_g,
                                      dense_w_gate, dense_w_up, dense_w_down, i)
        else:
            wr = jnp.zeros((D_MODEL, V7X_LANES), F32).at[:, :N_EXPERTS].set(w_router[i])
            wr_hi = wr.astype(BF16)
            wr_lo = (wr - wr_hi.astype(F32)).astype(BF16)
            wr_split = jnp.concatenate(
                [jnp.concatenate([wr_hi, wr_lo], axis=1),
                 jnp.concatenate([wr_hi, jnp.zeros_like(wr_lo)], axis=1)], axis=0)
            xm, route, counts = _mixer_out_router(attn, conv, xt, w_out, layer, ffn_g, wr_split)
            xt = _moe_layer(xm, route, counts, ffn_g, moe_w_gate, moe_w_up, moe_w_down, i)
    return xt.reshape(batch, seq, d)
```

```python
import functools

import jax
import jax.numpy as jnp
from jax import lax
from jax.experimental import pallas as pl
from jax.experimental.pallas import tpu as pltpu

F32 = jnp.float32
BF16 = jnp.bfloat16

D_MODEL = 1024
HEAD_DIM = 64
N_Q_HEADS = 8
N_KV_HEADS = 2
GQA_GROUP = N_Q_HEADS // N_KV_HEADS
ATTN_WIDTH = N_Q_HEADS * HEAD_DIM
KV_WIDTH = N_KV_HEADS * HEAD_DIM
QK_WIDTH = ATTN_WIDTH + KV_WIDTH
SEG_WIDTH = 256
BLOCK = 128
CONV_WIDTH = D_MODEL - ATTN_WIDTH
CONV_KERNEL = 31
IN_PROJ_WIDTH = ATTN_WIDTH + 2 * KV_WIDTH + 2 * CONV_WIDTH
N_EXPERTS = 8
NORM_EPS = 1e-6
LOG2_E = 1.4426950408889634

V7X_LANES = 128
V7X_SUBLANES = 8
V7X_VMEM_LIMIT_BYTES = 56 * 1024 * 1024
ROW_CHUNKS = D_MODEL // V7X_LANES

TM_PROJ = 512
ATTN_BLOCKS = 4
TC_CONV = 256
CONV_HALO = 32
CONV_SLABS = CONV_WIDTH // V7X_LANES
CONV_SEGS = 4
CONV_SEG_ROWS = TC_CONV // CONV_SEGS
CONV_SEG_LEN = CONV_SEG_ROWS + CONV_HALO
TM_FFN = 512
TF_DENSE = 256
TM_MOE = 512
TF_MOE = 512
TM_TOK = 512


def _params(sem, vmem=V7X_VMEM_LIMIT_BYTES):
    return pltpu.CompilerParams(dimension_semantics=sem, vmem_limit_bytes=vmem)


def _sigmoid(x):
    return 1.0 / (1.0 + jnp.exp(-x))


def _rms_norm_bf16(x, g):
    ms = jnp.mean(x * x, axis=-1, keepdims=True)
    return (x * lax.rsqrt(ms + NORM_EPS) * g).astype(BF16)


def _mixer_in_kernel(x_ref, g_ref, w_ref, gain_ref, seg_ref, q_ref, k_ref, v_ref, glu_ref, w_bf16):
    @pl.when(pl.program_id(0) == 0)
    def _():
        w_bf16[...] = w_ref[...].astype(BF16)

    h = _rms_norm_bf16(x_ref[...], g_ref[...])
    proj = jnp.dot(h, w_bf16[...], preferred_element_type=F32)
    qk = proj[:, :QK_WIDTH]
    sq = qk * qk
    hi = sq.astype(BF16)
    lo = (sq - hi.astype(F32)).astype(BF16)
    seg = seg_ref[...]
    parts = []
    for c0 in range(0, QK_WIDTH, SEG_WIDTH):
        width = min(SEG_WIDTH, QK_WIDTH - c0)
        blk = slice(c0, c0 + width)
        parts.append(jnp.dot(hi[:, blk], seg[:width, :width], preferred_element_type=F32)
                     + jnp.dot(lo[:, blk], seg[:width, :width], preferred_element_type=F32))
    ss = jnp.concatenate(parts, axis=1)
    qkn = qk * lax.rsqrt(ss * (1.0 / HEAD_DIM) + NORM_EPS) * gain_ref[...]
    q_ref[...] = qkn[:, :ATTN_WIDTH].astype(BF16)
    lane = lax.broadcasted_iota(jnp.int32, (x_ref.shape[0], KV_WIDTH), 1)
    low = lane < HEAD_DIM
    for src, dst in ((qkn[:, ATTN_WIDTH:QK_WIDTH], k_ref),
                     (proj[:, QK_WIDTH:QK_WIDTH + KV_WIDTH], v_ref)):
        rot = pltpu.roll(src, HEAD_DIM, axis=1)
        dst[:, :KV_WIDTH] = jnp.where(low, src, rot).astype(BF16)
        dst[:, KV_WIDTH:] = jnp.where(low, rot, src).astype(BF16)
    o_c = QK_WIDTH + KV_WIDTH
    u = proj[:, o_c:o_c + CONV_WIDTH]
    gate = proj[:, o_c + CONV_WIDTH:]
    glu_ref[...] = u * _sigmoid(gate)


def _mixer_in(x, g, w_in, layer, gain, seg):
    t = x.shape[0]
    tm = TM_PROJ
    row = lambda i: (i, 0)
    fixed = lambda i: (0, 0)
    return pl.pallas_call(
        _mixer_in_kernel,
        grid=(t // tm,),
        in_specs=[
            pl.BlockSpec((tm, D_MODEL), row),
            pl.BlockSpec((1, D_MODEL), fixed),
            pl.BlockSpec((None, D_MODEL, IN_PROJ_WIDTH), lambda i: (layer, 0, 0)),
            pl.BlockSpec((1, QK_WIDTH), fixed),
            pl.BlockSpec((SEG_WIDTH, SEG_WIDTH), fixed),
        ],
        out_specs=[
            pl.BlockSpec((tm, ATTN_WIDTH), row),
            pl.BlockSpec((tm, 2 * KV_WIDTH), row),
            pl.BlockSpec((tm, 2 * KV_WIDTH), row),
            pl.BlockSpec((tm, CONV_WIDTH), row),
        ],
        out_shape=[
            jax.ShapeDtypeStruct((t, ATTN_WIDTH), BF16),
            jax.ShapeDtypeStruct((t, 2 * KV_WIDTH), BF16),
            jax.ShapeDtypeStruct((t, 2 * KV_WIDTH), BF16),
            jax.ShapeDtypeStruct((t, CONV_WIDTH), F32),
        ],
        scratch_shapes=[pltpu.VMEM((D_MODEL, IN_PROJ_WIDTH), BF16)],
        compiler_params=_params(("arbitrary",)),
        name="mixer_in",
    )(x, g, w_in, gain, seg)


def _attn_kernel(q_ref, kc_ref, kp_ref, vc_ref, vp_ref, bias0_ref, bias_ref, o_ref):
    n_keys = (ATTN_BLOCKS + 1) * BLOCK
    low = lax.broadcasted_iota(jnp.int32, (n_keys, 2 * HEAD_DIM), 1) < HEAD_DIM
    low_q = lax.broadcasted_iota(jnp.int32, (BLOCK, 2 * HEAD_DIM), 1) < HEAD_DIM
    key0 = lax.broadcasted_iota(jnp.int32, (2 * BLOCK, 2 * HEAD_DIM), 0) == 0
    zero = jnp.zeros((), BF16)
    contract_last = (((1,), (1,)), ((), ()))
    for g in range(N_KV_HEADS):
        cols = slice(g * 2 * HEAD_DIM, (g + 1) * 2 * HEAD_DIM)
        k_all = jnp.concatenate([kp_ref[:, cols], kc_ref[:, cols]], axis=0)
        v_all = jnp.concatenate([vp_ref[:, cols], vc_ref[:, cols]], axis=0)
        k_lo_all = jnp.where(low, k_all, zero)
        k_hi_all = jnp.where(low, zero, k_all)
        pair0 = slice((2 * g) * 2 * HEAD_DIM, (2 * g + 1) * 2 * HEAD_DIM)
        pair1 = slice((2 * g + 1) * 2 * HEAD_DIM, (2 * g + 2) * 2 * HEAD_DIM)
        for blk in range(ATTN_BLOCKS):
            rows = slice(blk * BLOCK, (blk + 1) * BLOCK)
            band = slice(blk * BLOCK, (blk + 2) * BLOCK)
            b_ref = bias0_ref if blk == 0 else bias_ref
            k_lo = jnp.where(key0, zero, k_lo_all[band])
            k_hi = jnp.where(key0, zero, k_hi_all[band])
            v_band = jnp.where(key0, zero, v_all[band])
            v_ext = jnp.concatenate([v_band, jnp.ones_like(v_band)], axis=1)
            q2 = jnp.concatenate([q_ref[rows, pair0], q_ref[rows, pair1]], axis=0)
            s_lo = lax.dot_general(q2, k_lo, contract_last, preferred_element_type=F32)
            s_hi = lax.dot_general(q2, k_hi, contract_last, preferred_element_type=F32)
            sh = jnp.concatenate([s_lo, s_hi], axis=0) + b_ref[0, g]
            m = jnp.max(sh, axis=-1, keepdims=True)
            p = jnp.exp2(sh - m).astype(BF16)
            pv = jnp.dot(p, v_ext, preferred_element_type=F32)
            out = pv[:, :2 * HEAD_DIM] / pv[:, 2 * HEAD_DIM:]
            o_ref[rows, pair0] = jnp.where(low_q, out[:BLOCK], out[2 * BLOCK:3 * BLOCK]).astype(BF16)
            o_ref[rows, pair1] = jnp.where(low_q, out[BLOCK:2 * BLOCK], out[3 * BLOCK:]).astype(BF16)


def _attention(q, kd, vd, bias, batch, seq):
    t = q.shape[0]
    tq = ATTN_BLOCKS * BLOCK
    nt = seq // tq
    cur = lambda b, n: (b * nt + n, 0)
    prev = lambda b, n: ((b * nt + n) * ATTN_BLOCKS - jnp.minimum(n, 1), 0)
    bias_spec = lambda index_map: pl.BlockSpec(
        (1, N_KV_HEADS, GQA_GROUP * BLOCK, 2 * BLOCK), index_map)
    return pl.pallas_call(
        _attn_kernel,
        grid=(batch, nt),
        in_specs=[
            pl.BlockSpec((tq, ATTN_WIDTH), cur),
            pl.BlockSpec((tq, 2 * KV_WIDTH), cur),
            pl.BlockSpec((BLOCK, 2 * KV_WIDTH), prev),
            pl.BlockSpec((tq, 2 * KV_WIDTH), cur),
            pl.BlockSpec((BLOCK, 2 * KV_WIDTH), prev),
            bias_spec(lambda b, n: (jnp.minimum(n, 1), 0, 0, 0)),
            bias_spec(lambda b, n: (1, 0, 0, 0)),
        ],
        out_specs=pl.BlockSpec((tq, ATTN_WIDTH), cur),
        out_shape=jax.ShapeDtypeStruct((t, ATTN_WIDTH), BF16),
        compiler_params=_params(("arbitrary", "arbitrary")),
        name="attention",
    )(q, kd, kd, vd, vd, bias, bias)


def _attention_bias(sinks):
    heads = jnp.arange(1, N_Q_HEADS + 1, dtype=F32)
    slopes = jnp.exp2(-8.0 * heads / N_Q_HEADS)
    key = jnp.arange(2 * BLOCK)[None, :]
    dist = (jnp.arange(BLOCK)[:, None] + BLOCK) - key
    valid = (dist >= 0) & (dist < BLOCK)
    valid = jnp.stack([valid & (key >= BLOCK), valid])
    bias = -slopes[:, None, None] * dist.astype(F32)
    bias = jnp.where(valid[:, None], bias[None], -jnp.inf)
    bias = jnp.where(key == 0, sinks[None, :, None, None], bias) * LOG2_E
    order = jnp.array([[4 * g, 4 * g + 2, 4 * g + 1, 4 * g + 3] for g in range(N_KV_HEADS)])
    return bias[:, order].reshape(2, N_KV_HEADS, GQA_GROUP * BLOCK, 2 * BLOCK)


def _conv_kernel(tiles_per_seq, cur_ref, halo_ref, w_ref, b_ref, lng_ref, lnb_ref, o_ref, buf, ybuf):
    first = (pl.program_id(0) % tiles_per_seq) == 0
    groups = CONV_SEG_LEN // V7X_SUBLANES
    for s in range(CONV_SLABS):
        cols = slice(s * V7X_LANES, (s + 1) * V7X_LANES)
        for j in range(CONV_SEGS):
            for gi in range(groups):
                t0 = CONV_SEG_ROWS * j - CONV_HALO + V7X_SUBLANES * gi
                if t0 < 0:
                    h0 = CONV_HALO + t0
                    src = jnp.where(first, 0.0, halo_ref[h0:h0 + V7X_SUBLANES, cols])
                else:
                    src = cur_ref[t0:t0 + V7X_SUBLANES, cols]
                buf[s, pl.ds(CONV_SEGS * V7X_SUBLANES * gi + j, V7X_SUBLANES, stride=CONV_SEGS), :] = src

    base = CONV_HALO - (CONV_KERNEL - 1)
    out_groups = CONV_SEG_ROWS // V7X_SUBLANES

    def seg_body(it, carry):
        s = it // CONV_SEGS
        j = it % CONV_SEGS
        acc = [None] * out_groups
        for k in range(CONV_KERNEL):
            wk = w_ref[s, k]
            for gi in range(out_groups):
                i = V7X_SUBLANES * gi + base + k
                term = buf[s, pl.ds(CONV_SEGS * i + j, V7X_SUBLANES, stride=CONV_SEGS), :] * wk
                acc[gi] = term if acc[gi] is None else acc[gi] + term
        row0 = pl.multiple_of(j * CONV_SEG_ROWS, CONV_SEG_ROWS)
        ybuf[s, pl.ds(row0, CONV_SEG_ROWS), :] = jnp.concatenate(acc, axis=0)
        return carry

    lax.fori_loop(0, CONV_SLABS * CONV_SEGS, seg_body, 0, unroll=2)

    y = jnp.concatenate([ybuf[s] for s in range(CONV_SLABS)], axis=1) + b_ref[...]
    mu = jnp.mean(y, axis=-1, keepdims=True)
    yc = y - mu
    var = jnp.mean(yc * yc, axis=-1, keepdims=True)
    z = yc * lax.rsqrt(var + NORM_EPS) * lng_ref[...] + lnb_ref[...]
    o_ref[...] = (z * _sigmoid(z)).astype(BF16)


def _conv_weights(w):
    ws = w.reshape(CONV_KERNEL, CONV_SLABS, V7X_LANES).transpose(1, 0, 2)
    return jnp.broadcast_to(ws[:, :, None, :], (CONV_SLABS, CONV_KERNEL, V7X_SUBLANES, V7X_LANES))


def _conv(glu, w_slab, b, lng, lnb, seq):
    t = glu.shape[0]
    tc = TC_CONV
    ratio = tc // CONV_HALO
    fixed = lambda i: (0, 0)
    return pl.pallas_call(
        functools.partial(_conv_kernel, seq // tc),
        grid=(t // tc,),
        in_specs=[
            pl.BlockSpec((tc, CONV_WIDTH), lambda i: (i, 0)),
            pl.BlockSpec((CONV_HALO, CONV_WIDTH), lambda i: (jnp.maximum(i * ratio - 1, 0), 0)),
            pl.BlockSpec((CONV_SLABS, CONV_KERNEL, V7X_SUBLANES, V7X_LANES), lambda i: (0, 0, 0, 0)),
            pl.BlockSpec((1, CONV_WIDTH), fixed),
            pl.BlockSpec((1, CONV_WIDTH), fixed),
            pl.BlockSpec((1, CONV_WIDTH), fixed),
        ],
        out_specs=pl.BlockSpec((tc, CONV_WIDTH), lambda i: (i, 0)),
        out_shape=jax.ShapeDtypeStruct((t, CONV_WIDTH), BF16),
        scratch_shapes=[pltpu.VMEM((CONV_SLABS, CONV_SEGS * CONV_SEG_LEN, V7X_LANES), F32),
                        pltpu.VMEM((CONV_SLABS, tc, V7X_LANES), F32)],
        compiler_params=_params(("arbitrary",)),
        name="conv",
    )(glu, glu, w_slab, b, lng, lnb)


def _cast_weight_once(w_ref, w_bf16):
    @pl.when(pl.program_id(0) == 0)
    def _():
        w_bf16[...] = w_ref[...].astype(BF16)


def _out_proj_residual(attn_ref, conv_ref, x_ref, w_bf16):
    y = (jnp.dot(attn_ref[...], w_bf16[:ATTN_WIDTH], preferred_element_type=F32)
         + jnp.dot(conv_ref[...], w_bf16[ATTN_WIDTH:], preferred_element_type=F32))
    return x_ref[...] + y


def _mixer_out_router_kernel(attn_ref, conv_ref, x_ref, w_ref, g_ref, wr_ref,
                             o_ref, route_ref, count_ref, w_bf16, tri, carry):
    i = pl.program_id(0)
    tm = x_ref.shape[0]

    @pl.when(i == 0)
    def _():
        carry[...] = jnp.zeros_like(carry)
        earlier = (lax.broadcasted_iota(jnp.int32, (tm, tm), 0)
                   < lax.broadcasted_iota(jnp.int32, (tm, tm), 1))
        tri[...] = jnp.where(earlier, 1.0, 0.0).astype(BF16)

    _cast_weight_once(w_ref, w_bf16)
    x = _out_proj_residual(attn_ref, conv_ref, x_ref, w_bf16)
    o_ref[...] = x

    ms = jnp.mean(x * x, axis=-1, keepdims=True)
    h = x * lax.rsqrt(ms + NORM_EPS) * g_ref[...]
    h_hi = h.astype(BF16)
    h_lo = (h - h_hi.astype(F32)).astype(BF16)
    both = jnp.dot(jnp.concatenate([h_hi, h_lo], axis=1), wr_ref[...],
                   preferred_element_type=F32)
    logits = both[:, :V7X_LANES] + both[:, V7X_LANES:]

    lt = jnp.transpose(logits)[:N_EXPERTS]
    sub = lax.broadcasted_iota(jnp.int32, (N_EXPERTS, tm), 0)
    sub_f = sub.astype(F32)
    neg_inf = -jnp.inf
    none = float(N_EXPERTS)
    m1 = jnp.max(lt, axis=0, keepdims=True)
    i1 = jnp.min(jnp.where(lt == m1, sub_f, none), axis=0, keepdims=True)
    oh1 = sub_f == i1
    lt2 = jnp.where(oh1, neg_inf, lt)
    m2 = jnp.max(lt2, axis=0, keepdims=True)
    i2 = jnp.min(jnp.where(lt2 == m2, sub_f, none), axis=0, keepdims=True)
    oh2 = sub_f == i2
    e = jnp.exp(m2 - m1)
    w1 = 1.0 / (1.0 + e)
    w2 = e / (1.0 + e)

    member = jnp.where(oh1 | oh2, 1.0, 0.0)
    before = (jnp.dot(member.astype(BF16), tri[...], preferred_element_type=F32)
              + carry[:, 0:1])
    rank1 = jnp.sum(jnp.where(oh1, before, 0.0), axis=0, keepdims=True)
    rank2 = jnp.sum(jnp.where(oh2, before, 0.0), axis=0, keepdims=True)
    carry[...] = carry[...] + jnp.sum(member, axis=1, keepdims=True)

    route = jnp.where(sub == 0, i1, 0.0)
    route = jnp.where(sub == 1, i2, route)
    route = jnp.where(sub == 2, w1, route)
    route = jnp.where(sub == 3, w2, route)
    route = jnp.where(sub == 4, rank1, route)
    route = jnp.where(sub == 5, rank2, route)
    route_ref[0] = route
    count_ref[...] = carry[...]


def _mixer_out_router(attn, conv, x, w_out, layer, g, wr_split):
    t = x.shape[0]
    tm = TM_PROJ
    row = lambda i: (i, 0)
    fixed = lambda i: (0, 0)
    in_specs = [
        pl.BlockSpec((tm, ATTN_WIDTH), row),
        pl.BlockSpec((tm, CONV_WIDTH), row),
        pl.BlockSpec((tm, D_MODEL), row),
        pl.BlockSpec((None, D_MODEL, D_MODEL), lambda i: (layer, 0, 0)),
    ]
    w_scratch = pltpu.VMEM((D_MODEL, D_MODEL), BF16)
    return pl.pallas_call(
        _mixer_out_router_kernel,
        grid=(t // tm,),
        in_specs=in_specs + [
            pl.BlockSpec((1, D_MODEL), fixed),
            pl.BlockSpec((2 * D_MODEL, 2 * V7X_LANES), fixed),
        ],
        out_specs=[
            pl.BlockSpec((tm, D_MODEL), row),
            pl.BlockSpec((1, N_EXPERTS, tm), lambda i: (i, 0, 0)),
            pl.BlockSpec((N_EXPERTS, V7X_LANES), fixed),
        ],
        out_shape=[
            jax.ShapeDtypeStruct((t, D_MODEL), F32),
            jax.ShapeDtypeStruct((t // tm, N_EXPERTS, tm), F32),
            jax.ShapeDtypeStruct((N_EXPERTS, V7X_LANES), F32),
        ],
        scratch_shapes=[w_scratch, pltpu.VMEM((tm, tm), BF16),
                        pltpu.VMEM((N_EXPERTS, V7X_LANES), F32)],
        compiler_params=_params(("arbitrary",)),
        name="mixer_out_router",
    )(attn, conv, x, w_out, g, wr_split)


def _ffn_chunk_copies(w_hbm, lead, j, tf, stage, sem, slot):
    cols = pl.ds(j * tf, tf)
    every = slice(None)
    index = (lead + (every, cols), lead + (every, cols), lead + (cols, every))
    return [pltpu.make_async_copy(w_hbm[m].at[index[m]], stage[m].at[slot], sem.at[m, slot])
            for m in range(3)]


def _prefetch_weights(w_hbm, lead, tf, stage, sem):
    for j in range(2):
        for c in _ffn_chunk_copies(w_hbm, lead, j, tf, stage, sem, j):
            c.start()


def _swiglu_chunk(h, res, j):
    gate = jnp.dot(h, res[0][j], preferred_element_type=F32)
    up = jnp.dot(h, res[1][j], preferred_element_type=F32)
    a = (gate * _sigmoid(gate) * up).astype(BF16)
    return jnp.dot(a, res[2][j], preferred_element_type=F32)


def _swiglu_tile(load_weights, prepare, emit, w_hbm, lead, res, stage, sem, hbf, acc, nj, tf):
    @pl.when(load_weights)
    def _():
        h, ctx = prepare()
        hbf[...] = h
        for j in range(nj):
            slot = j % 2
            for c in _ffn_chunk_copies(w_hbm, lead, j, tf, stage, sem, slot):
                c.wait()
            for m in range(3):
                res[m][j] = stage[m][slot].astype(BF16)
            if j + 2 < nj:
                for c in _ffn_chunk_copies(w_hbm, lead, j + 2, tf, stage, sem, slot):
                    c.start()
            yj = _swiglu_chunk(hbf[...], res, j)
            if j == 0:
                acc[...] = yj
            else:
                acc[...] += yj
        emit(acc[...], ctx)

    @pl.when(jnp.logical_not(load_weights))
    def _():
        h, ctx = prepare()
        y = _swiglu_chunk(h, res, 0)
        for j in range(1, nj):
            y = y + _swiglu_chunk(h, res, j)
        emit(y, ctx)


def _ffn_scratch(tm, nj, tf):
    assert nj >= 2
    return [
        pltpu.VMEM((nj, D_MODEL, tf), BF16),
        pltpu.VMEM((nj, D_MODEL, tf), BF16),
        pltpu.VMEM((nj, tf, D_MODEL), BF16),
        pltpu.VMEM((2, D_MODEL, tf), F32),
        pltpu.VMEM((2, D_MODEL, tf), F32),
        pltpu.VMEM((2, tf, D_MODEL), F32),
        pltpu.SemaphoreType.DMA((3, 2)),
        pltpu.VMEM((tm, D_MODEL), BF16),
        pltpu.VMEM((tm, D_MODEL), F32),
    ]


def _dense_ffn_kernel(ffn_layer, nj, tf, attn_ref, conv_ref, x_ref, w_out_ref, g_ref, wg, wu, wd,
                      o_ref, w_out_bf16, rg, ru, rd, sg, su, sd, sem, hbf, acc):
    first = pl.program_id(0) == 0

    @pl.when(first)
    def _():
        _prefetch_weights((wg, wu, wd), (ffn_layer,), tf, (sg, su, sd), sem)

    _cast_weight_once(w_out_ref, w_out_bf16)

    def prepare():
        x_mid = _out_proj_residual(attn_ref, conv_ref, x_ref, w_out_bf16)
        return _rms_norm_bf16(x_mid, g_ref[...]), x_mid

    def emit(y, x_mid):
        o_ref[...] = x_mid + y

    _swiglu_tile(first, prepare, emit, (wg, wu, wd), (ffn_layer,), (rg, ru, rd),
                 (sg, su, sd), sem, hbf, acc, nj, tf)


def _mixer_out_dense_ffn(attn, conv, x, w_out, layer, g, wg, wu, wd, ffn_layer):
    t = x.shape[0]
    tm = TM_FFN
    tf = TF_DENSE
    nj = wg.shape[2] // tf
    row = lambda i: (i, 0)
    hbm = pl.BlockSpec(memory_space=pl.ANY)
    return pl.pallas_call(
        functools.partial(_dense_ffn_kernel, ffn_layer, nj, tf),
        grid=(t // tm,),
        in_specs=[
            pl.BlockSpec((tm, ATTN_WIDTH), row),
            pl.BlockSpec((tm, CONV_WIDTH), row),
            pl.BlockSpec((tm, D_MODEL), row),
            pl.BlockSpec((None, D_MODEL, D_MODEL), lambda i: (layer, 0, 0)),
            pl.BlockSpec((1, D_MODEL), lambda i: (0, 0)),
            hbm, hbm, hbm,
        ],
        out_specs=pl.BlockSpec((tm, D_MODEL), row),
        out_shape=jax.ShapeDtypeStruct((t, D_MODEL), F32),
        scratch_shapes=[pltpu.VMEM((D_MODEL, D_MODEL), BF16)] + _ffn_scratch(tm, nj, tf),
        compiler_params=_params(("arbitrary",)),
        name="dense_ffn",
    )(attn, conv, x, w_out, g, wg, wu, wd)


def _token_rows(ref, row):
    return ref.at[pl.ds(pl.multiple_of(row * ROW_CHUNKS, ROW_CHUNKS), ROW_CHUNKS), :]


def _dispatch_kernel(fill_ref, pos_ref, x_ref, g_ref, xs_ref, buf, sem, fill_sem):
    tm = x_ref.shape[0]

    @pl.when(pl.program_id(0) == 0)
    def _():
        buf[...] = jnp.zeros_like(buf)
        tile_len = TM_MOE * ROW_CHUNKS

        def fill(n):
            start = pl.multiple_of(fill_ref[n + 1] * tile_len, tile_len)
            return pltpu.make_async_copy(buf.at[pl.ds(0, tile_len), :],
                                         xs_ref.at[pl.ds(start, tile_len), :], fill_sem)

        def start_fill(n, carry):
            fill(n).start()
            return carry

        def wait_fill(n, carry):
            fill(n).wait()
            return carry

        lax.fori_loop(0, fill_ref[0], start_fill, 0)
        lax.fori_loop(0, fill_ref[0], wait_fill, 0)

    x = x_ref[...]
    ms = jnp.mean(x * x, axis=-1, keepdims=True)
    h = x * lax.rsqrt(ms + NORM_EPS) * g_ref[...]
    for c in range(ROW_CHUNKS):
        buf[pl.ds(c, tm, stride=ROW_CHUNKS), :] = h[:, c * V7X_LANES:(c + 1) * V7X_LANES]

    def issue(r, carry):
        src = _token_rows(buf, r)
        for k in range(2):
            dst = _token_rows(xs_ref, pos_ref[0, 0, 2 * r + k])
            pltpu.make_async_copy(src, dst, sem).start(priority=k)
        return carry

    lax.fori_loop(0, tm, issue, 0)
    for k in range(2):
        pltpu.make_async_copy(buf, xs_ref.at[pl.ds(0, tm * ROW_CHUNKS), :], sem).wait()


def _dispatch(fill_tiles, pos_tiles, x, g, n_rows):
    t = x.shape[0]
    tm = TM_TOK
    assert tm >= TM_MOE
    grid_spec = pltpu.PrefetchScalarGridSpec(
        num_scalar_prefetch=1,
        grid=(t // tm,),
        in_specs=[
            pl.BlockSpec((1, 1, 2 * tm), lambda i, fill: (i, 0, 0), memory_space=pltpu.SMEM),
            pl.BlockSpec((tm, D_MODEL), lambda i, fill: (i, 0)),
            pl.BlockSpec((1, D_MODEL), lambda i, fill: (0, 0)),
        ],
        out_specs=pl.BlockSpec(memory_space=pl.ANY),
        scratch_shapes=[pltpu.VMEM((tm * ROW_CHUNKS, V7X_LANES), F32),
                        pltpu.SemaphoreType.DMA(()),
                        pltpu.SemaphoreType.DMA(())],
    )
    return pl.pallas_call(
        _dispatch_kernel,
        grid_spec=grid_spec,
        out_shape=jax.ShapeDtypeStruct((n_rows * ROW_CHUNKS, V7X_LANES), F32),
        compiler_params=_params(("arbitrary",)),
        name="moe_dispatch",
    )(fill_tiles, pos_tiles, x, g)


def _moe_ffn_kernel(layer, nj, tf, te_ref, tr_ref, tl_ref, xs_ref, wg, wu, wd, ys_ref,
                    rg, ru, rd, sg, su, sd, sem, hbf, acc):
    i = pl.program_id(0)
    tm = hbf.shape[0]
    rows = tr_ref[i]
    w_hbm = (wg, wu, wd)
    stage = (sg, su, sd)

    loads = tl_ref[i] != 0
    next_loads = tl_ref[i + 1] != 0

    @pl.when((i == 0) & loads)
    def _():
        _prefetch_weights(w_hbm, (layer, te_ref[0]), tf, stage, sem)

    @pl.when(next_loads & jnp.logical_not(loads))
    def _():
        _prefetch_weights(w_hbm, (layer, te_ref[i + 1]), tf, stage, sem)

    @pl.when(rows > 0)
    def _():
        def prepare():
            x = jnp.concatenate(
                [xs_ref[pl.ds(c, tm, stride=ROW_CHUNKS), :] for c in range(ROW_CHUNKS)], axis=1)
            return x.astype(BF16), None

        def emit(y, _):
            for c in range(ROW_CHUNKS):
                ys_ref[pl.ds(c, tm, stride=ROW_CHUNKS), :] = y[:, c * V7X_LANES:(c + 1) * V7X_LANES]

        _swiglu_tile(loads, prepare, emit, w_hbm, (layer, te_ref[i]), (rg, ru, rd),
                     stage, sem, hbf, acc, nj, tf)

    @pl.when(rows == 0)
    def _():
        ys_ref[...] = jnp.zeros_like(ys_ref)

    @pl.when(next_loads & loads)
    def _():
        _prefetch_weights(w_hbm, (layer, te_ref[i + 1]), tf, stage, sem)


def _moe_ffn(tile_expert, tile_rows, tile_load, xs, wg, wu, wd, layer):
    n_tiles = tile_rows.shape[0]
    assert tile_expert.shape[0] == n_tiles + 1 and tile_load.shape[0] == n_tiles + 1
    tm = TM_MOE
    tf = TF_MOE
    nj = wg.shape[3] // tf
    hbm = pl.BlockSpec(memory_space=pl.ANY)
    grid_spec = pltpu.PrefetchScalarGridSpec(
        num_scalar_prefetch=3,
        grid=(n_tiles,),
        in_specs=[
            pl.BlockSpec((tm * ROW_CHUNKS, V7X_LANES), lambda i, te, tr, tl: (i, 0)),
            hbm, hbm, hbm,
        ],
        out_specs=pl.BlockSpec((tm * ROW_CHUNKS, V7X_LANES), lambda i, te, tr, tl: (i, 0)),
        scratch_shapes=_ffn_scratch(tm, nj, tf),
    )
    return pl.pallas_call(
        functools.partial(_moe_ffn_kernel, layer, nj, tf),
        grid_spec=grid_spec,
        out_shape=jax.ShapeDtypeStruct(xs.shape, F32),
        compiler_params=_params(("arbitrary",)),
        name="moe_ffn",
    )(tile_expert, tile_rows, tile_load, xs, wg, wu, wd)


def _combine_kernel(pos_ref, x_ref, w_ref, ys_ref, o_ref, y1buf, y2buf, sem):
    tm = x_ref.shape[0]
    half = tm // 2

    def issue(h):
        def body(r, carry):
            for k, ybuf in ((0, y1buf), (1, y2buf)):
                src = _token_rows(ys_ref, pos_ref[0, 0, 2 * r + k])
                pltpu.make_async_copy(src, _token_rows(ybuf, r), sem.at[h]).start(priority=k)
            return carry
        lax.fori_loop(h * half, (h + 1) * half, body, 0, unroll=4)

    issue(0)
    issue(1)
    for h in range(2):
        buf_rows = pl.ds(h * half * ROW_CHUNKS, half * ROW_CHUNKS)
        for ybuf in (y1buf, y2buf):
            pltpu.make_async_copy(ys_ref.at[pl.ds(0, half * ROW_CHUNKS), :], ybuf.at[buf_rows, :],
                                  sem.at[h]).wait()
        tok = slice(h * half, (h + 1) * half)
        w1 = w_ref[tok, 0:1]
        w2 = w_ref[tok, 1:2]
        for c in range(ROW_CHUNKS):
            cols = slice(c * V7X_LANES, (c + 1) * V7X_LANES)
            y1 = y1buf[pl.ds(h * half * ROW_CHUNKS + c, half, stride=ROW_CHUNKS), :]
            y2 = y2buf[pl.ds(h * half * ROW_CHUNKS + c, half, stride=ROW_CHUNKS), :]
            o_ref[tok, cols] = x_ref[tok, cols] + (w1 * y1 + w2 * y2)


def _combine(pos_tiles, x, gate_w, ys):
    t = x.shape[0]
    tm = TM_TOK
    return pl.pallas_call(
        _combine_kernel,
        grid=(t // tm,),
        in_specs=[
            pl.BlockSpec((1, 1, 2 * tm), lambda i: (i, 0, 0), memory_space=pltpu.SMEM),
            pl.BlockSpec((tm, D_MODEL), lambda i: (i, 0)),
            pl.BlockSpec((tm, 2), lambda i: (i, 0)),
            pl.BlockSpec(memory_space=pl.ANY),
        ],
        out_specs=pl.BlockSpec((tm, D_MODEL), lambda i: (i, 0)),
        out_shape=jax.ShapeDtypeStruct((t, D_MODEL), F32),
        scratch_shapes=[pltpu.VMEM((tm * ROW_CHUNKS, V7X_LANES), F32),
                        pltpu.VMEM((tm * ROW_CHUNKS, V7X_LANES), F32),
                        pltpu.SemaphoreType.DMA((2,))],
        compiler_params=_params(("arbitrary",)),
        name="moe_combine",
    )(pos_tiles, x, gate_w, ys)


def _moe_layer(x, route, counts, g, wg, wu, wd, layer):
    t = x.shape[0]
    tm = TM_MOE
    n_tiles = (2 * t) // tm + N_EXPERTS
    per_token = lambda rows: rows.transpose(0, 2, 1).reshape(t, 2)
    expert = per_token(route[:, 0:2]).astype(jnp.int32)
    gate_w = per_token(route[:, 2:4])
    rank = per_token(route[:, 4:6]).astype(jnp.int32)
    count = counts[:, 0].astype(jnp.int32)
    padded = ((count + tm - 1) // tm) * tm
    gend = jnp.cumsum(padded)
    gstart = gend - padded
    ids = jnp.arange(N_EXPERTS, dtype=jnp.int32)
    pos = rank + jnp.sum(jnp.where(expert[..., None] == ids, gstart, 0), axis=-1)
    tile_start = jnp.arange(n_tiles, dtype=jnp.int32) * tm
    last_expert = jnp.max(jnp.where(padded > 0, ids, 0))
    tile_expert = jnp.minimum(
        jnp.sum((tile_start[:, None] >= gend[None, :]).astype(jnp.int32), axis=1), last_expert)
    used_end = (gstart + count)[tile_expert]
    tile_rows = jnp.clip(used_end - tile_start, 0, tm)
    prev_expert = jnp.concatenate([jnp.full((1,), -1, jnp.int32), tile_expert[:-1]])
    tile_load = ((tile_rows > 0) & (tile_expert != prev_expert)).astype(jnp.int32)
    pos_tiles = pos.reshape(t // TM_TOK, 1, 2 * TM_TOK)

    max_fill = n_tiles - (2 * t - N_EXPERTS * (tm - 1)) // tm
    partial = tile_rows < tm
    fill_idx = jnp.nonzero(partial, size=max_fill, fill_value=0)[0].astype(jnp.int32)
    fill_tiles = jnp.concatenate([jnp.sum(partial, dtype=jnp.int32)[None], fill_idx])

    xs = _dispatch(fill_tiles, pos_tiles, x, g, n_tiles * tm)
    one_more = lambda v, last: jnp.concatenate([v, jnp.full((1,), last, jnp.int32)])
    ys = _moe_ffn(one_more(tile_expert, 0), tile_rows, one_more(tile_load, 0), xs,
                  wg, wu, wd, layer)
    return _combine(pos_tiles, x, gate_w, ys)


def kernel(x, attn_norm_g, w_in, q_norm_g, k_norm_g, sinks, conv_w, conv_b, conv_ln_g,
           conv_ln_b, w_out, ffn_norm_g, dense_w_gate, dense_w_up, dense_w_down,
           w_router, moe_w_gate, moe_w_up, moe_w_down):
    batch, seq, d = x.shape
    depth = w_in.shape[0]
    t = batch * seq
    assert d == D_MODEL and seq % TC_CONV == 0 and t % TM_PROJ == 0 and t % TM_TOK == 0
    assert seq % (ATTN_BLOCKS * BLOCK) == 0
    assert w_in.shape[2] == IN_PROJ_WIDTH and conv_w.shape[1] == CONV_KERNEL
    assert dense_w_gate.shape[2] % TF_DENSE == 0 and moe_w_gate.shape[3] % TF_MOE == 0

    xt = x.reshape(t, d)
    head_id = jnp.arange(SEG_WIDTH) // HEAD_DIM
    seg = (head_id[:, None] == head_id[None, :]).astype(BF16)
    scale = HEAD_DIM ** -0.5 * LOG2_E

    for layer in range(depth):
        gain = jnp.concatenate([jnp.tile(q_norm_g[layer], N_Q_HEADS) * scale,
                                jnp.tile(k_norm_g[layer], N_KV_HEADS)])[None, :]
        q, kd, vd, glu = _mixer_in(xt, attn_norm_g[layer][None, :], w_in, layer, gain, seg)
        attn = _attention(q, kd, vd, _attention_bias(sinks[layer]), batch, seq)
        conv = _conv(glu, _conv_weights(conv_w[layer]), conv_b[layer][None, :],
                     conv_ln_g[layer][None, :],
                     conv_ln_b[layer][None, :], seq)
        ffn_g = ffn_norm_g[layer][None, :]
        i = layer // 2
        if layer % 2 == 0:
            xt = _mixer_out_dense_ffn(attn, conv, xt, w_out, layer, ffn_g,
                                      dense_w_gate, dense_w_up, dense_w_down, i)
        else:
            wr = jnp.zeros((D_MODEL, V7X_LANES), F32).at[:, :N_EXPERTS].set(w_router[i])
            wr_hi = wr.astype(BF16)
            wr_lo = (wr - wr_hi.astype(F32)).astype(BF16)
            wr_split = jnp.concatenate(
                [jnp.concatenate([wr_hi, wr_lo], axis=1),
                 jnp.concatenate([wr_hi, jnp.zeros_like(wr_lo)], axis=1)], axis=0)
            xm, route, counts = _mixer_out_router(attn, conv, xt, w_out, layer, ffn_g, wr_split)
            xt = _moe_layer(xm, route, counts, ffn_g, moe_w_gate, moe_w_up, moe_w_down, i)
    return xt.reshape(batch, seq, d)
```

```python
import functools

import jax
import jax.numpy as jnp
from jax import lax
from jax.experimental import pallas as pl
from jax.experimental.pallas import tpu as pltpu

F32 = jnp.float32
BF16 = jnp.bfloat16

D_MODEL = 1024
HEAD_DIM = 64
N_Q_HEADS = 8
N_KV_HEADS = 2
GQA_GROUP = N_Q_HEADS // N_KV_HEADS
ATTN_WIDTH = N_Q_HEADS * HEAD_DIM
KV_WIDTH = N_KV_HEADS * HEAD_DIM
QK_WIDTH = ATTN_WIDTH + KV_WIDTH
SEG_WIDTH = 256
BLOCK = 128
CONV_WIDTH = D_MODEL - ATTN_WIDTH
CONV_KERNEL = 31
IN_PROJ_WIDTH = ATTN_WIDTH + 2 * KV_WIDTH + 2 * CONV_WIDTH
N_EXPERTS = 8
NORM_EPS = 1e-6
LOG2_E = 1.4426950408889634

V7X_LANES = 128
V7X_SUBLANES = 8
V7X_VMEM_LIMIT_BYTES = 56 * 1024 * 1024
ROW_CHUNKS = D_MODEL // V7X_LANES

TM_PROJ = 1024
MIXER_IN_PARTS = 2
ATTN_BLOCKS = 8
TC_CONV = 256
CONV_HALO = 32
CONV_SLABS = CONV_WIDTH // V7X_LANES
CONV_SEGS = 4
CONV_SEG_ROWS = TC_CONV // CONV_SEGS
CONV_SEG_LEN = CONV_SEG_ROWS + CONV_HALO
TM_FFN = 512
TF_DENSE = 256
TM_MOE = 512
TF_MOE = 512
TM_TOK = 512


def _params(sem, vmem=V7X_VMEM_LIMIT_BYTES):
    return pltpu.CompilerParams(dimension_semantics=sem, vmem_limit_bytes=vmem)


def _sigmoid(x):
    return 1.0 / (1.0 + jnp.exp(-x))


def _rms_norm_bf16(x, g):
    ms = jnp.mean(x * x, axis=-1, keepdims=True)
    return (x * lax.rsqrt(ms + NORM_EPS) * g).astype(BF16)


def _mixer_in_kernel(x_ref, g_ref, w_ref, gain_ref, seg_ref, q_ref, k_ref, v_ref, glu_ref, w_bf16):
    @pl.when(pl.program_id(0) == 0)
    def _():
        w_bf16[...] = w_ref[...].astype(BF16)

    part = x_ref.shape[0] // MIXER_IN_PARTS
    seg = seg_ref[...]
    low = lax.broadcasted_iota(jnp.int32, (part, KV_WIDTH), 1) < HEAD_DIM
    for p in range(MIXER_IN_PARTS):
        rows = slice(p * part, (p + 1) * part)
        h = _rms_norm_bf16(x_ref[rows], g_ref[...])
        proj = jnp.dot(h, w_bf16[...], preferred_element_type=F32)
        qk = proj[:, :QK_WIDTH]
        sq = qk * qk
        hi = sq.astype(BF16)
        lo = (sq - hi.astype(F32)).astype(BF16)
        parts = []
        for c0 in range(0, QK_WIDTH, SEG_WIDTH):
            width = min(SEG_WIDTH, QK_WIDTH - c0)
            blk = slice(c0, c0 + width)
            parts.append(jnp.dot(hi[:, blk], seg[:width, :width], preferred_element_type=F32)
                         + jnp.dot(lo[:, blk], seg[:width, :width], preferred_element_type=F32))
        ss = jnp.concatenate(parts, axis=1)
        qkn = qk * lax.rsqrt(ss * (1.0 / HEAD_DIM) + NORM_EPS) * gain_ref[...]
        q_ref[rows] = qkn[:, :ATTN_WIDTH].astype(BF16)
        for src, dst in ((qkn[:, ATTN_WIDTH:QK_WIDTH], k_ref),
                         (proj[:, QK_WIDTH:QK_WIDTH + KV_WIDTH], v_ref)):
            rot = pltpu.roll(src, HEAD_DIM, axis=1)
            dst[rows, :KV_WIDTH] = jnp.where(low, src, rot).astype(BF16)
            dst[rows, KV_WIDTH:] = jnp.where(low, rot, src).astype(BF16)
        o_c = QK_WIDTH + KV_WIDTH
        u = proj[:, o_c:o_c + CONV_WIDTH]
        gate = proj[:, o_c + CONV_WIDTH:]
        glu_ref[rows] = u * _sigmoid(gate)


def _mixer_in(x, g, w_in, layer, gain, seg):
    t = x.shape[0]
    tm = TM_PROJ
    row = lambda i: (i, 0)
    fixed = lambda i: (0, 0)
    return pl.pallas_call(
        _mixer_in_kernel,
        grid=(t // tm,),
        in_specs=[
            pl.BlockSpec((tm, D_MODEL), row),
            pl.BlockSpec((1, D_MODEL), fixed),
            pl.BlockSpec((None, D_MODEL, IN_PROJ_WIDTH), lambda i: (layer, 0, 0)),
            pl.BlockSpec((1, QK_WIDTH), fixed),
            pl.BlockSpec((SEG_WIDTH, SEG_WIDTH), fixed),
        ],
        out_specs=[
            pl.BlockSpec((tm, ATTN_WIDTH), row),
            pl.BlockSpec((tm, 2 * KV_WIDTH), row),
            pl.BlockSpec((tm, 2 * KV_WIDTH), row),
            pl.BlockSpec((tm, CONV_WIDTH), row),
        ],
        out_shape=[
            jax.ShapeDtypeStruct((t, ATTN_WIDTH), BF16),
            jax.ShapeDtypeStruct((t, 2 * KV_WIDTH), BF16),
            jax.ShapeDtypeStruct((t, 2 * KV_WIDTH), BF16),
            jax.ShapeDtypeStruct((t, CONV_WIDTH), F32),
        ],
        scratch_shapes=[pltpu.VMEM((D_MODEL, IN_PROJ_WIDTH), BF16)],
        compiler_params=_params(("arbitrary",)),
        name="mixer_in",
    )(x, g, w_in, gain, seg)


def _attn_kernel(q_ref, kc_ref, kp_ref, vc_ref, vp_ref, bias0_ref, bias_ref, o_ref):
    n_keys = (ATTN_BLOCKS + 1) * BLOCK
    low = lax.broadcasted_iota(jnp.int32, (n_keys, 2 * HEAD_DIM), 1) < HEAD_DIM
    low_q = lax.broadcasted_iota(jnp.int32, (BLOCK, 2 * HEAD_DIM), 1) < HEAD_DIM
    key0 = lax.broadcasted_iota(jnp.int32, (2 * BLOCK, 2 * HEAD_DIM), 0) == 0
    zero = jnp.zeros((), BF16)
    contract_last = (((1,), (1,)), ((), ()))
    for g in range(N_KV_HEADS):
        cols = slice(g * 2 * HEAD_DIM, (g + 1) * 2 * HEAD_DIM)
        k_all = jnp.concatenate([kp_ref[:, cols], kc_ref[:, cols]], axis=0)
        v_all = jnp.concatenate([vp_ref[:, cols], vc_ref[:, cols]], axis=0)
        k_lo_all = jnp.where(low, k_all, zero)
        k_hi_all = jnp.where(low, zero, k_all)
        pair0 = slice((2 * g) * 2 * HEAD_DIM, (2 * g + 1) * 2 * HEAD_DIM)
        pair1 = slice((2 * g + 1) * 2 * HEAD_DIM, (2 * g + 2) * 2 * HEAD_DIM)
        for blk in range(ATTN_BLOCKS):
            rows = slice(blk * BLOCK, (blk + 1) * BLOCK)
            band = slice(blk * BLOCK, (blk + 2) * BLOCK)
            b_ref = bias0_ref if blk == 0 else bias_ref
            k_lo = jnp.where(key0, zero, k_lo_all[band])
            k_hi = jnp.where(key0, zero, k_hi_all[band])
            v_band = jnp.where(key0, zero, v_all[band])
            v_ext = jnp.concatenate([v_band, jnp.ones_like(v_band)], axis=1)
            q2 = jnp.concatenate([q_ref[rows, pair0], q_ref[rows, pair1]], axis=0)
            s_lo = lax.dot_general(q2, k_lo, contract_last, preferred_element_type=F32)
            s_hi = lax.dot_general(q2, k_hi, contract_last, preferred_element_type=F32)
            sh = jnp.concatenate([s_lo, s_hi], axis=0) + b_ref[0, g]
            m = jnp.max(sh, axis=-1, keepdims=True)
            p = jnp.exp2(sh - m).astype(BF16)
            pv = jnp.dot(p, v_ext, preferred_element_type=F32)
            out = pv[:, :2 * HEAD_DIM] / pv[:, 2 * HEAD_DIM:]
            o_ref[rows, pair0] = jnp.where(low_q, out[:BLOCK], out[2 * BLOCK:3 * BLOCK]).astype(BF16)
            o_ref[rows, pair1] = jnp.where(low_q, out[BLOCK:2 * BLOCK], out[3 * BLOCK:]).astype(BF16)


def _attention(q, kd, vd, bias, batch, seq):
    t = q.shape[0]
    tq = ATTN_BLOCKS * BLOCK
    nt = seq // tq
    cur = lambda b, n: (b * nt + n, 0)
    prev = lambda b, n: ((b * nt + n) * ATTN_BLOCKS - jnp.minimum(n, 1), 0)
    bias_spec = lambda index_map: pl.BlockSpec(
        (1, N_KV_HEADS, GQA_GROUP * BLOCK, 2 * BLOCK), index_map)
    return pl.pallas_call(
        _attn_kernel,
        grid=(batch, nt),
        in_specs=[
            pl.BlockSpec((tq, ATTN_WIDTH), cur),
            pl.BlockSpec((tq, 2 * KV_WIDTH), cur),
            pl.BlockSpec((BLOCK, 2 * KV_WIDTH), prev),
            pl.BlockSpec((tq, 2 * KV_WIDTH), cur),
            pl.BlockSpec((BLOCK, 2 * KV_WIDTH), prev),
            bias_spec(lambda b, n: (jnp.minimum(n, 1), 0, 0, 0)),
            bias_spec(lambda b, n: (1, 0, 0, 0)),
        ],
        out_specs=pl.BlockSpec((tq, ATTN_WIDTH), cur),
        out_shape=jax.ShapeDtypeStruct((t, ATTN_WIDTH), BF16),
        compiler_params=_params(("arbitrary", "arbitrary")),
        name="attention",
    )(q, kd, kd, vd, vd, bias, bias)


def _attention_bias(sinks):
    heads = jnp.arange(1, N_Q_HEADS + 1, dtype=F32)
    slopes = jnp.exp2(-8.0 * heads / N_Q_HEADS)
    key = jnp.arange(2 * BLOCK)[None, :]
    dist = (jnp.arange(BLOCK)[:, None] + BLOCK) - key
    valid = (dist >= 0) & (dist < BLOCK)
    valid = jnp.stack([valid & (key >= BLOCK), valid])
    bias = -slopes[:, None, None] * dist.astype(F32)
    bias = jnp.where(valid[:, None], bias[None], -jnp.inf)
    bias = jnp.where(key == 0, sinks[None, :, None, None], bias) * LOG2_E
    order = jnp.array([[4 * g, 4 * g + 2, 4 * g + 1, 4 * g + 3] for g in range(N_KV_HEADS)])
    return bias[:, order].reshape(2, N_KV_HEADS, GQA_GROUP * BLOCK, 2 * BLOCK)


def _conv_kernel(tiles_per_seq, cur_ref, halo_ref, w_ref, b_ref, lng_ref, lnb_ref, o_ref, buf, ybuf):
    first = (pl.program_id(0) % tiles_per_seq) == 0
    groups = CONV_SEG_LEN // V7X_SUBLANES
    for s in range(CONV_SLABS):
        cols = slice(s * V7X_LANES, (s + 1) * V7X_LANES)
        for j in range(CONV_SEGS):
            for gi in range(groups):
                t0 = CONV_SEG_ROWS * j - CONV_HALO + V7X_SUBLANES * gi
                if t0 < 0:
                    h0 = CONV_HALO + t0
                    src = jnp.where(first, 0.0, halo_ref[h0:h0 + V7X_SUBLANES, cols])
                else:
                    src = cur_ref[t0:t0 + V7X_SUBLANES, cols]
                buf[s, pl.ds(CONV_SEGS * V7X_SUBLANES * gi + j, V7X_SUBLANES, stride=CONV_SEGS), :] = src

    base = CONV_HALO - (CONV_KERNEL - 1)
    out_groups = CONV_SEG_ROWS // V7X_SUBLANES

    def seg_body(it, carry):
        s = it // CONV_SEGS
        j = it % CONV_SEGS
        acc = [None] * out_groups
        for k in range(CONV_KERNEL):
            wk = w_ref[s, k]
            for gi in range(out_groups):
                i = V7X_SUBLANES * gi + base + k
                term = buf[s, pl.ds(CONV_SEGS * i + j, V7X_SUBLANES, stride=CONV_SEGS), :] * wk
                acc[gi] = term if acc[gi] is None else acc[gi] + term
        row0 = pl.multiple_of(j * CONV_SEG_ROWS, CONV_SEG_ROWS)
        ybuf[s, pl.ds(row0, CONV_SEG_ROWS), :] = jnp.concatenate(acc, axis=0)
        return carry

    lax.fori_loop(0, CONV_SLABS * CONV_SEGS, seg_body, 0, unroll=2)

    y = jnp.concatenate([ybuf[s] for s in range(CONV_SLABS)], axis=1) + b_ref[...]
    mu = jnp.mean(y, axis=-1, keepdims=True)
    yc = y - mu
    var = jnp.mean(yc * yc, axis=-1, keepdims=True)
    z = yc * lax.rsqrt(var + NORM_EPS) * lng_ref[...] + lnb_ref[...]
    o_ref[...] = (z * _sigmoid(z)).astype(BF16)


def _conv_weights(w):
    ws = w.reshape(CONV_KERNEL, CONV_SLABS, V7X_LANES).transpose(1, 0, 2)
    return jnp.broadcast_to(ws[:, :, None, :], (CONV_SLABS, CONV_KERNEL, V7X_SUBLANES, V7X_LANES))


def _conv(glu, w_slab, b, lng, lnb, seq):
    t = glu.shape[0]
    tc = TC_CONV
    ratio = tc // CONV_HALO
    fixed = lambda i: (0, 0)
    return pl.pallas_call(
        functools.partial(_conv_kernel, seq // tc),
        grid=(t // tc,),
        in_specs=[
            pl.BlockSpec((tc, CONV_WIDTH), lambda i: (i, 0)),
            pl.BlockSpec((CONV_HALO, CONV_WIDTH), lambda i: (jnp.maximum(i * ratio - 1, 0), 0)),
            pl.BlockSpec((CONV_SLABS, CONV_KERNEL, V7X_SUBLANES, V7X_LANES), lambda i: (0, 0, 0, 0)),
            pl.BlockSpec((1, CONV_WIDTH), fixed),
            pl.BlockSpec((1, CONV_WIDTH), fixed),
            pl.BlockSpec((1, CONV_WIDTH), fixed),
        ],
        out_specs=pl.BlockSpec((tc, CONV_WIDTH), lambda i: (i, 0)),
        out_shape=jax.ShapeDtypeStruct((t, CONV_WIDTH), BF16),
        scratch_shapes=[pltpu.VMEM((CONV_SLABS, CONV_SEGS * CONV_SEG_LEN, V7X_LANES), F32),
                        pltpu.VMEM((CONV_SLABS, tc, V7X_LANES), F32)],
        compiler_params=_params(("arbitrary",)),
        name="conv",
    )(glu, glu, w_slab, b, lng, lnb)


def _cast_weight_once(w_ref, w_bf16):
    @pl.when(pl.program_id(0) == 0)
    def _():
        w_bf16[...] = w_ref[...].astype(BF16)


def _out_proj_residual(attn_ref, conv_ref, x_ref, w_bf16):
    y = (jnp.dot(attn_ref[...], w_bf16[:ATTN_WIDTH], preferred_element_type=F32)
         + jnp.dot(conv_ref[...], w_bf16[ATTN_WIDTH:], preferred_element_type=F32))
    return x_ref[...] + y


def _mixer_out_router_kernel(attn_ref, conv_ref, x_ref, w_ref, g_ref, wr_ref,
                             o_ref, route_ref, count_ref, w_bf16, tri, carry):
    i = pl.program_id(0)
    tm = x_ref.shape[0]

    @pl.when(i == 0)
    def _():
        carry[...] = jnp.zeros_like(carry)
        earlier = (lax.broadcasted_iota(jnp.int32, (tm, tm), 0)
                   < lax.broadcasted_iota(jnp.int32, (tm, tm), 1))
        tri[...] = jnp.where(earlier, 1.0, 0.0).astype(BF16)

    _cast_weight_once(w_ref, w_bf16)
    x = _out_proj_residual(attn_ref, conv_ref, x_ref, w_bf16)
    o_ref[...] = x

    ms = jnp.mean(x * x, axis=-1, keepdims=True)
    h = x * lax.rsqrt(ms + NORM_EPS) * g_ref[...]
    h_hi = h.astype(BF16)
    h_lo = (h - h_hi.astype(F32)).astype(BF16)
    both = jnp.dot(jnp.concatenate([h_hi, h_lo], axis=1), wr_ref[...],
                   preferred_element_type=F32)
    logits = both[:, :V7X_LANES] + both[:, V7X_LANES:]

    lt = jnp.transpose(logits)[:N_EXPERTS]
    sub = lax.broadcasted_iota(jnp.int32, (N_EXPERTS, tm), 0)
    sub_f = sub.astype(F32)
    neg_inf = -jnp.inf
    none = float(N_EXPERTS)
    m1 = jnp.max(lt, axis=0, keepdims=True)
    i1 = jnp.min(jnp.where(lt == m1, sub_f, none), axis=0, keepdims=True)
    oh1 = sub_f == i1
    lt2 = jnp.where(oh1, neg_inf, lt)
    m2 = jnp.max(lt2, axis=0, keepdims=True)
    i2 = jnp.min(jnp.where(lt2 == m2, sub_f, none), axis=0, keepdims=True)
    oh2 = sub_f == i2
    e = jnp.exp(m2 - m1)
    w1 = 1.0 / (1.0 + e)
    w2 = e / (1.0 + e)

    member = jnp.where(oh1 | oh2, 1.0, 0.0)
    before = (jnp.dot(member.astype(BF16), tri[...], preferred_element_type=F32)
              + carry[:, 0:1])
    rank1 = jnp.sum(jnp.where(oh1, before, 0.0), axis=0, keepdims=True)
    rank2 = jnp.sum(jnp.where(oh2, before, 0.0), axis=0, keepdims=True)
    carry[...] = carry[...] + jnp.sum(member, axis=1, keepdims=True)

    route = jnp.where(sub == 0, i1, 0.0)
    route = jnp.where(sub == 1, i2, route)
    route = jnp.where(sub == 2, w1, route)
    route = jnp.where(sub == 3, w2, route)
    route = jnp.where(sub == 4, rank1, route)
    route = jnp.where(sub == 5, rank2, route)
    route_ref[0] = route
    count_ref[...] = carry[...]


def _mixer_out_router(attn, conv, x, w_out, layer, g, wr_split):
    t = x.shape[0]
    tm = TM_PROJ
    row = lambda i: (i, 0)
    fixed = lambda i: (0, 0)
    in_specs = [
        pl.BlockSpec((tm, ATTN_WIDTH), row),
        pl.BlockSpec((tm, CONV_WIDTH), row),
        pl.BlockSpec((tm, D_MODEL), row),
        pl.BlockSpec((None, D_MODEL, D_MODEL), lambda i: (layer, 0, 0)),
    ]
    w_scratch = pltpu.VMEM((D_MODEL, D_MODEL), BF16)
    return pl.pallas_call(
        _mixer_out_router_kernel,
        grid=(t // tm,),
        in_specs=in_specs + [
            pl.BlockSpec((1, D_MODEL), fixed),
            pl.BlockSpec((2 * D_MODEL, 2 * V7X_LANES), fixed),
        ],
        out_specs=[
            pl.BlockSpec((tm, D_MODEL), row),
            pl.BlockSpec((1, N_EXPERTS, tm), lambda i: (i, 0, 0)),
            pl.BlockSpec((N_EXPERTS, V7X_LANES), fixed),
        ],
        out_shape=[
            jax.ShapeDtypeStruct((t, D_MODEL), F32),
            jax.ShapeDtypeStruct((t // tm, N_EXPERTS, tm), F32),
            jax.ShapeDtypeStruct((N_EXPERTS, V7X_LANES), F32),
        ],
        scratch_shapes=[w_scratch, pltpu.VMEM((tm, tm), BF16),
                        pltpu.VMEM((N_EXPERTS, V7X_LANES), F32)],
        compiler_params=_params(("arbitrary",)),
        name="mixer_out_router",
    )(attn, conv, x, w_out, g, wr_split)


def _ffn_chunk_copies(w_hbm, lead, j, tf, stage, sem, slot):
    cols = pl.ds(j * tf, tf)
    every = slice(None)
    index = (lead + (every, cols), lead + (every, cols), lead + (cols, every))
    return [pltpu.make_async_copy(w_hbm[m].at[index[m]], stage[m].at[slot], sem.at[m, slot])
            for m in range(3)]


def _prefetch_weights(w_hbm, lead, tf, stage, sem):
    for j in range(2):
        for c in _ffn_chunk_copies(w_hbm, lead, j, tf, stage, sem, j):
            c.start()


def _swiglu_chunk(h, res, j):
    gate = jnp.dot(h, res[0][j], preferred_element_type=F32)
    up = jnp.dot(h, res[1][j], preferred_element_type=F32)
    a = (gate * _sigmoid(gate) * up).astype(BF16)
    return jnp.dot(a, res[2][j], preferred_element_type=F32)


def _swiglu_tile(load_weights, prepare, emit, w_hbm, lead, res, stage, sem, hbf, acc, nj, tf):
    @pl.when(load_weights)
    def _():
        h, ctx = prepare()
        hbf[...] = h
        for j in range(nj):
            slot = j % 2
            for c in _ffn_chunk_copies(w_hbm, lead, j, tf, stage, sem, slot):
                c.wait()
            for m in range(3):
                res[m][j] = stage[m][slot].astype(BF16)
            if j + 2 < nj:
                for c in _ffn_chunk_copies(w_hbm, lead, j + 2, tf, stage, sem, slot):
                    c.start()
            yj = _swiglu_chunk(hbf[...], res, j)
            if j == 0:
                acc[...] = yj
            else:
                acc[...] += yj
        emit(acc[...], ctx)

    @pl.when(jnp.logical_not(load_weights))
    def _():
        h, ctx = prepare()
        y = _swiglu_chunk(h, res, 0)
        for j in range(1, nj):
            y = y + _swiglu_chunk(h, res, j)
        emit(y, ctx)


def _ffn_scratch(tm, nj, tf):
    assert nj >= 2
    return [
        pltpu.VMEM((nj, D_MODEL, tf), BF16),
        pltpu.VMEM((nj, D_MODEL, tf), BF16),
        pltpu.VMEM((nj, tf, D_MODEL), BF16),
        pltpu.VMEM((2, D_MODEL, tf), F32),
        pltpu.VMEM((2, D_MODEL, tf), F32),
        pltpu.VMEM((2, tf, D_MODEL), F32),
        pltpu.SemaphoreType.DMA((3, 2)),
        pltpu.VMEM((tm, D_MODEL), BF16),
        pltpu.VMEM((tm, D_MODEL), F32),
    ]


def _dense_ffn_kernel(ffn_layer, nj, tf, attn_ref, conv_ref, x_ref, w_out_ref, g_ref, wg, wu, wd,
                      o_ref, w_out_bf16, rg, ru, rd, sg, su, sd, sem, hbf, acc):
    first = pl.program_id(0) == 0

    @pl.when(first)
    def _():
        _prefetch_weights((wg, wu, wd), (ffn_layer,), tf, (sg, su, sd), sem)

    _cast_weight_once(w_out_ref, w_out_bf16)

    def prepare():
        x_mid = _out_proj_residual(attn_ref, conv_ref, x_ref, w_out_bf16)
        return _rms_norm_bf16(x_mid, g_ref[...]), x_mid

    def emit(y, x_mid):
        o_ref[...] = x_mid + y

    _swiglu_tile(first, prepare, emit, (wg, wu, wd), (ffn_layer,), (rg, ru, rd),
                 (sg, su, sd), sem, hbf, acc, nj, tf)


def _mixer_out_dense_ffn(attn, conv, x, w_out, layer, g, wg, wu, wd, ffn_layer):
    t = x.shape[0]
    tm = TM_FFN
    tf = TF_DENSE
    nj = wg.shape[2] // tf
    row = lambda i: (i, 0)
    hbm = pl.BlockSpec(memory_space=pl.ANY)
    return pl.pallas_call(
        functools.partial(_dense_ffn_kernel, ffn_layer, nj, tf),
        grid=(t // tm,),
        in_specs=[
            pl.BlockSpec((tm, ATTN_WIDTH), row),
            pl.BlockSpec((tm, CONV_WIDTH), row),
            pl.BlockSpec((tm, D_MODEL), row),
            pl.BlockSpec((None, D_MODEL, D_MODEL), lambda i: (layer, 0, 0)),
            pl.BlockSpec((1, D_MODEL), lambda i: (0, 0)),
            hbm, hbm, hbm,
        ],
        out_specs=pl.BlockSpec((tm, D_MODEL), row),
        out_shape=jax.ShapeDtypeStruct((t, D_MODEL), F32),
        scratch_shapes=[pltpu.VMEM((D_MODEL, D_MODEL), BF16)] + _ffn_scratch(tm, nj, tf),
        compiler_params=_params(("arbitrary",)),
        name="dense_ffn",
    )(attn, conv, x, w_out, g, wg, wu, wd)


def _token_rows(ref, row):
    return ref.at[pl.ds(pl.multiple_of(row * ROW_CHUNKS, ROW_CHUNKS), ROW_CHUNKS), :]


def _dispatch_kernel(fill_ref, pos_ref, x_ref, g_ref, xs_ref, buf, sem, fill_sem):
    tm = x_ref.shape[0]

    @pl.when(pl.program_id(0) == 0)
    def _():
        buf[...] = jnp.zeros_like(buf)
        tile_len = TM_MOE * ROW_CHUNKS

        def fill(n):
            start = pl.multiple_of(fill_ref[n + 1] * tile_len, tile_len)
            return pltpu.make_async_copy(buf.at[pl.ds(0, tile_len), :],
                                         xs_ref.at[pl.ds(start, tile_len), :], fill_sem)

        def start_fill(n, carry):
            fill(n).start()
            return carry

        def wait_fill(n, carry):
            fill(n).wait()
            return carry

        lax.fori_loop(0, fill_ref[0], start_fill, 0)
        lax.fori_loop(0, fill_ref[0], wait_fill, 0)

    x = x_ref[...]
    ms = jnp.mean(x * x, axis=-1, keepdims=True)
    h = x * lax.rsqrt(ms + NORM_EPS) * g_ref[...]
    for c in range(ROW_CHUNKS):
        buf[pl.ds(c, tm, stride=ROW_CHUNKS), :] = h[:, c * V7X_LANES:(c + 1) * V7X_LANES]

    def issue(r, carry):
        src = _token_rows(buf, r)
        for k in range(2):
            dst = _token_rows(xs_ref, pos_ref[0, 0, 2 * r + k])
            pltpu.make_async_copy(src, dst, sem).start(priority=k)
        return carry

    lax.fori_loop(0, tm, issue, 0)
    for k in range(2):
        pltpu.make_async_copy(buf, xs_ref.at[pl.ds(0, tm * ROW_CHUNKS), :], sem).wait()


def _dispatch(fill_tiles, pos_tiles, x, g, n_rows):
    t = x.shape[0]
    tm = TM_TOK
    assert tm >= TM_MOE
    grid_spec = pltpu.PrefetchScalarGridSpec(
        num_scalar_prefetch=1,
        grid=(t // tm,),
        in_specs=[
            pl.BlockSpec((1, 1, 2 * tm), lambda i, fill: (i, 0, 0), memory_space=pltpu.SMEM),
            pl.BlockSpec((tm, D_MODEL), lambda i, fill: (i, 0)),
            pl.BlockSpec((1, D_MODEL), lambda i, fill: (0, 0)),
        ],
        out_specs=pl.BlockSpec(memory_space=pl.ANY),
        scratch_shapes=[pltpu.VMEM((tm * ROW_CHUNKS, V7X_LANES), F32),
                        pltpu.SemaphoreType.DMA(()),
                        pltpu.SemaphoreType.DMA(())],
    )
    return pl.pallas_call(
        _dispatch_kernel,
        grid_spec=grid_spec,
        out_shape=jax.ShapeDtypeStruct((n_rows * ROW_CHUNKS, V7X_LANES), F32),
        compiler_params=_params(("arbitrary",)),
        name="moe_dispatch",
    )(fill_tiles, pos_tiles, x, g)


def _moe_ffn_kernel(layer, nj, tf, te_ref, tr_ref, tl_ref, xs_ref, wg, wu, wd, ys_ref,
                    rg, ru, rd, sg, su, sd, sem, hbf, acc):
    i = pl.program_id(0)
    tm = hbf.shape[0]
    rows = tr_ref[i]
    w_hbm = (wg, wu, wd)
    stage = (sg, su, sd)

    loads = tl_ref[i] != 0
    next_loads = tl_ref[i + 1] != 0

    @pl.when((i == 0) & loads)
    def _():
        _prefetch_weights(w_hbm, (layer, te_ref[0]), tf, stage, sem)

    @pl.when(next_loads & jnp.logical_not(loads))
    def _():
        _prefetch_weights(w_hbm, (layer, te_ref[i + 1]), tf, stage, sem)

    @pl.when(rows > 0)
    def _():
        def prepare():
            x = jnp.concatenate(
                [xs_ref[pl.ds(c, tm, stride=ROW_CHUNKS), :] for c in range(ROW_CHUNKS)], axis=1)
            return x.astype(BF16), None

        def emit(y, _):
            for c in range(ROW_CHUNKS):
                ys_ref[pl.ds(c, tm, stride=ROW_CHUNKS), :] = y[:, c * V7X_LANES:(c + 1) * V7X_LANES]

        _swiglu_tile(loads, prepare, emit, w_hbm, (layer, te_ref[i]), (rg, ru, rd),
                     stage, sem, hbf, acc, nj, tf)

    @pl.when(rows == 0)
    def _():
        ys_ref[...] = jnp.zeros_like(ys_ref)

    @pl.when(next_loads & loads)
    def _():
        _prefetch_weights(w_hbm, (layer, te_ref[i + 1]), tf, stage, sem)


def _moe_ffn(tile_expert, tile_rows, tile_load, xs, wg, wu, wd, layer):
    n_tiles = tile_rows.shape[0]
    assert tile_expert.shape[0] == n_tiles + 1 and tile_load.shape[0] == n_tiles + 1
    tm = TM_MOE
    tf = TF_MOE
    nj = wg.shape[3] // tf
    hbm = pl.BlockSpec(memory_space=pl.ANY)
    grid_spec = pltpu.PrefetchScalarGridSpec(
        num_scalar_prefetch=3,
        grid=(n_tiles,),
        in_specs=[
            pl.BlockSpec((tm * ROW_CHUNKS, V7X_LANES), lambda i, te, tr, tl: (i, 0)),
            hbm, hbm, hbm,
        ],
        out_specs=pl.BlockSpec((tm * ROW_CHUNKS, V7X_LANES), lambda i, te, tr, tl: (i, 0)),
        scratch_shapes=_ffn_scratch(tm, nj, tf),
    )
    return pl.pallas_call(
        functools.partial(_moe_ffn_kernel, layer, nj, tf),
        grid_spec=grid_spec,
        out_shape=jax.ShapeDtypeStruct(xs.shape, F32),
        compiler_params=_params(("arbitrary",)),
        name="moe_ffn",
    )(tile_expert, tile_rows, tile_load, xs, wg, wu, wd)


def _combine_kernel(pos_ref, x_ref, w_ref, ys_ref, o_ref, y1buf, y2buf, sem):
    tm = x_ref.shape[0]
    half = tm // 2

    def issue(h):
        def body(r, carry):
            for k, ybuf in ((0, y1buf), (1, y2buf)):
                src = _token_rows(ys_ref, pos_ref[0, 0, 2 * r + k])
                pltpu.make_async_copy(src, _token_rows(ybuf, r), sem.at[h]).start(priority=k)
            return carry
        lax.fori_loop(h * half, (h + 1) * half, body, 0, unroll=4)

    issue(0)
    issue(1)
    for h in range(2):
        buf_rows = pl.ds(h * half * ROW_CHUNKS, half * ROW_CHUNKS)
        for ybuf in (y1buf, y2buf):
            pltpu.make_async_copy(ys_ref.at[pl.ds(0, half * ROW_CHUNKS), :], ybuf.at[buf_rows, :],
                                  sem.at[h]).wait()
        tok = slice(h * half, (h + 1) * half)
        w1 = w_ref[tok, 0:1]
        w2 = w_ref[tok, 1:2]
        for c in range(ROW_CHUNKS):
            cols = slice(c * V7X_LANES, (c + 1) * V7X_LANES)
            y1 = y1buf[pl.ds(h * half * ROW_CHUNKS + c, half, stride=ROW_CHUNKS), :]
            y2 = y2buf[pl.ds(h * half * ROW_CHUNKS + c, half, stride=ROW_CHUNKS), :]
            o_ref[tok, cols] = x_ref[tok, cols] + (w1 * y1 + w2 * y2)


def _combine(pos_tiles, x, gate_w, ys):
    t = x.shape[0]
    tm = TM_TOK
    return pl.pallas_call(
        _combine_kernel,
        grid=(t // tm,),
        in_specs=[
            pl.BlockSpec((1, 1, 2 * tm), lambda i: (i, 0, 0), memory_space=pltpu.SMEM),
            pl.BlockSpec((tm, D_MODEL), lambda i: (i, 0)),
            pl.BlockSpec((tm, 2), lambda i: (i, 0)),
            pl.BlockSpec(memory_space=pl.ANY),
        ],
        out_specs=pl.BlockSpec((tm, D_MODEL), lambda i: (i, 0)),
        out_shape=jax.ShapeDtypeStruct((t, D_MODEL), F32),
        scratch_shapes=[pltpu.VMEM((tm * ROW_CHUNKS, V7X_LANES), F32),
                        pltpu.VMEM((tm * ROW_CHUNKS, V7X_LANES), F32),
                        pltpu.SemaphoreType.DMA((2,))],
        compiler_params=_params(("arbitrary",)),
        name="moe_combine",
    )(pos_tiles, x, gate_w, ys)


def _moe_layer(x, route, counts, g, wg, wu, wd, layer):
    t = x.shape[0]
    tm = TM_MOE
    n_tiles = (2 * t) // tm + N_EXPERTS
    per_token = lambda rows: rows.transpose(0, 2, 1).reshape(t, 2)
    expert = per_token(route[:, 0:2]).astype(jnp.int32)
    gate_w = per_token(route[:, 2:4])
    rank = per_token(route[:, 4:6]).astype(jnp.int32)
    count = counts[:, 0].astype(jnp.int32)
    padded = ((count + tm - 1) // tm) * tm
    gend = jnp.cumsum(padded)
    gstart = gend - padded
    ids = jnp.arange(N_EXPERTS, dtype=jnp.int32)
    pos = rank + jnp.sum(jnp.where(expert[..., None] == ids, gstart, 0), axis=-1)
    tile_start = jnp.arange(n_tiles, dtype=jnp.int32) * tm
    last_expert = jnp.max(jnp.where(padded > 0, ids, 0))
    tile_expert = jnp.minimum(
        jnp.sum((tile_start[:, None] >= gend[None, :]).astype(jnp.int32), axis=1), last_expert)
    used_end = (gstart + count)[tile_expert]
    tile_rows = jnp.clip(used_end - tile_start, 0, tm)
    prev_expert = jnp.concatenate([jnp.full((1,), -1, jnp.int32), tile_expert[:-1]])
    tile_load = ((tile_rows > 0) & (tile_expert != prev_expert)).astype(jnp.int32)
    pos_tiles = pos.reshape(t // TM_TOK, 1, 2 * TM_TOK)

    max_fill = n_tiles - (2 * t - N_EXPERTS * (tm - 1)) // tm
    partial = tile_rows < tm
    fill_idx = jnp.nonzero(partial, size=max_fill, fill_value=0)[0].astype(jnp.int32)
    fill_tiles = jnp.concatenate([jnp.sum(partial, dtype=jnp.int32)[None], fill_idx])

    xs = _dispatch(fill_tiles, pos_tiles, x, g, n_tiles * tm)
    one_more = lambda v, last: jnp.concatenate([v, jnp.full((1,), last, jnp.int32)])
    ys = _moe_ffn(one_more(tile_expert, 0), tile_rows, one_more(tile_load, 0), xs,
                  wg, wu, wd, layer)
    return _combine(pos_tiles, x, gate_w, ys)


def kernel(x, attn_norm_g, w_in, q_norm_g, k_norm_g, sinks, conv_w, conv_b, conv_ln_g,
           conv_ln_b, w_out, ffn_norm_g, dense_w_gate, dense_w_up, dense_w_down,
           w_router, moe_w_gate, moe_w_up, moe_w_down):
    batch, seq, d = x.shape
    depth = w_in.shape[0]
    t = batch * seq
    assert d == D_MODEL and seq % TC_CONV == 0 and t % TM_PROJ == 0 and t % TM_TOK == 0
    assert seq % (ATTN_BLOCKS * BLOCK) == 0
    assert w_in.shape[2] == IN_PROJ_WIDTH and conv_w.shape[1] == CONV_KERNEL
    assert dense_w_gate.shape[2] % TF_DENSE == 0 and moe_w_gate.shape[3] % TF_MOE == 0

    xt = x.reshape(t, d)
    head_id = jnp.arange(SEG_WIDTH) // HEAD_DIM
    seg = (head_id[:, None] == head_id[None, :]).astype(BF16)
    scale = HEAD_DIM ** -0.5 * LOG2_E

    for layer in range(depth):
        gain = jnp.concatenate([jnp.tile(q_norm_g[layer], N_Q_HEADS) * scale,
                                jnp.tile(k_norm_g[layer], N_KV_HEADS)])[None, :]
        q, kd, vd, glu = _mixer_in(xt, attn_norm_g[layer][None, :], w_in, layer, gain, seg)
        attn = _attention(q, kd, vd, _attention_bias(sinks[layer]), batch, seq)
        conv = _conv(glu, _conv_weights(conv_w[layer]), conv_b[layer][None, :],
                     conv_ln_g[layer][None, :],
                     conv_ln_b[layer][None, :], seq)
        ffn_g = ffn_norm_g[layer][None, :]
        i = layer // 2
        if layer % 2 == 0:
            xt = _mixer_out_dense_ffn(attn, conv, xt, w_out, layer, ffn_g,
                                      dense_w_gate, dense_w_up, dense_w_down, i)
        else:
            wr = jnp.zeros((D_MODEL, V7X_LANES), F32).at[:, :N_EXPERTS].set(w_router[i])
            wr_hi = wr.astype(BF16)
            wr_lo = (wr - wr_hi.astype(F32)).astype(BF16)
            wr_split = jnp.concatenate(
                [jnp.concatenate([wr_hi, wr_lo], axis=1),
                 jnp.concatenate([wr_hi, jnp.zeros_like(wr_lo)], axis=1)], axis=0)
            xm, route, counts = _mixer_out_router(attn, conv, xt, w_out, layer, ffn_g, wr_split)
            xt = _moe_layer(xm, route, counts, ffn_g, moe_w_gate, moe_w_up, moe_w_down, i)
    return xt.reshape(batch, seq, d)
```

```python
import functools

import jax
import jax.numpy as jnp
from jax import lax
from jax.experimental import pallas as pl
from jax.experimental.pallas import tpu as pltpu

F32 = jnp.float32
BF16 = jnp.bfloat16

D_MODEL = 1024
HEAD_DIM = 64
N_Q_HEADS = 8
N_KV_HEADS = 2
GQA_GROUP = N_Q_HEADS // N_KV_HEADS
ATTN_WIDTH = N_Q_HEADS * HEAD_DIM
KV_WIDTH = N_KV_HEADS * HEAD_DIM
QK_WIDTH = ATTN_WIDTH + KV_WIDTH
SEG_WIDTH = 256
BLOCK = 128
CONV_WIDTH = D_MODEL - ATTN_WIDTH
CONV_KERNEL = 31
IN_PROJ_WIDTH = ATTN_WIDTH + 2 * KV_WIDTH + 2 * CONV_WIDTH
N_EXPERTS = 8
NORM_EPS = 1e-6
LOG2_E = 1.4426950408889634

V7X_LANES = 128
V7X_SUBLANES = 8
V7X_VMEM_LIMIT_BYTES = 56 * 1024 * 1024
ROW_CHUNKS = D_MODEL // V7X_LANES

TM_PROJ = 1024
MIXER_IN_PARTS = 2
ATTN_BLOCKS = 8
TC_CONV = 256
CONV_HALO = 32
CONV_SLABS = CONV_WIDTH // V7X_LANES
CONV_SEGS = 4
CONV_SEG_ROWS = TC_CONV // CONV_SEGS
CONV_SEG_LEN = CONV_SEG_ROWS + CONV_HALO
TM_FFN = 512
TF_DENSE = 256
TM_MOE = 512
TF_MOE = 512
TM_TOK = 512


def _params(sem, vmem=V7X_VMEM_LIMIT_BYTES):
    return pltpu.CompilerParams(dimension_semantics=sem, vmem_limit_bytes=vmem)


def _sigmoid(x):
    return 1.0 / (1.0 + jnp.exp(-x))


def _rms_norm_bf16(x, g):
    ms = jnp.mean(x * x, axis=-1, keepdims=True)
    return (x * lax.rsqrt(ms + NORM_EPS) * g).astype(BF16)


def _mixer_in_kernel(x_ref, g_ref, w_ref, gain_ref, seg_ref, q_ref, k_ref, v_ref, glu_ref, w_bf16):
    @pl.when(pl.program_id(0) == 0)
    def _():
        w_bf16[...] = w_ref[...].astype(BF16)

    part = x_ref.shape[0] // MIXER_IN_PARTS
    seg = seg_ref[...]
    low = lax.broadcasted_iota(jnp.int32, (part, KV_WIDTH), 1) < HEAD_DIM
    for p in range(MIXER_IN_PARTS):
        rows = slice(p * part, (p + 1) * part)
        h = _rms_norm_bf16(x_ref[rows], g_ref[...])
        proj = jnp.dot(h, w_bf16[...], preferred_element_type=F32)
        qk = proj[:, :QK_WIDTH]
        sq = qk * qk
        hi = sq.astype(BF16)
        lo = (sq - hi.astype(F32)).astype(BF16)
        parts = []
        for c0 in range(0, QK_WIDTH, SEG_WIDTH):
            width = min(SEG_WIDTH, QK_WIDTH - c0)
            blk = slice(c0, c0 + width)
            parts.append(jnp.dot(hi[:, blk], seg[:width, :width], preferred_element_type=F32)
                         + jnp.dot(lo[:, blk], seg[:width, :width], preferred_element_type=F32))
        ss = jnp.concatenate(parts, axis=1)
        qkn = qk * lax.rsqrt(ss * (1.0 / HEAD_DIM) + NORM_EPS) * gain_ref[...]
        q_ref[rows] = qkn[:, :ATTN_WIDTH].astype(BF16)
        for src, dst in ((qkn[:, ATTN_WIDTH:QK_WIDTH], k_ref),
                         (proj[:, QK_WIDTH:QK_WIDTH + KV_WIDTH], v_ref)):
            rot = pltpu.roll(src, HEAD_DIM, axis=1)
            dst[rows, :KV_WIDTH] = jnp.where(low, src, rot).astype(BF16)
            dst[rows, KV_WIDTH:] = jnp.where(low, rot, src).astype(BF16)
        o_c = QK_WIDTH + KV_WIDTH
        u = proj[:, o_c:o_c + CONV_WIDTH]
        gate = proj[:, o_c + CONV_WIDTH:]
        glu_ref[rows] = u * _sigmoid(gate)


def _mixer_in(x, g, w_in, layer, gain, seg):
    t = x.shape[0]
    tm = TM_PROJ
    row = lambda i: (i, 0)
    fixed = lambda i: (0, 0)
    return pl.pallas_call(
        _mixer_in_kernel,
        grid=(t // tm,),
        in_specs=[
            pl.BlockSpec((tm, D_MODEL), row),
            pl.BlockSpec((1, D_MODEL), fixed),
            pl.BlockSpec((None, D_MODEL, IN_PROJ_WIDTH), lambda i: (layer, 0, 0)),
            pl.BlockSpec((1, QK_WIDTH), fixed),
            pl.BlockSpec((SEG_WIDTH, SEG_WIDTH), fixed),
        ],
        out_specs=[
            pl.BlockSpec((tm, ATTN_WIDTH), row),
            pl.BlockSpec((tm, 2 * KV_WIDTH), row),
            pl.BlockSpec((tm, 2 * KV_WIDTH), row),
            pl.BlockSpec((tm, CONV_WIDTH), row),
        ],
        out_shape=[
            jax.ShapeDtypeStruct((t, ATTN_WIDTH), BF16),
            jax.ShapeDtypeStruct((t, 2 * KV_WIDTH), BF16),
            jax.ShapeDtypeStruct((t, 2 * KV_WIDTH), BF16),
            jax.ShapeDtypeStruct((t, CONV_WIDTH), F32),
        ],
        scratch_shapes=[pltpu.VMEM((D_MODEL, IN_PROJ_WIDTH), BF16)],
        compiler_params=_params(("arbitrary",)),
        name="mixer_in",
    )(x, g, w_in, gain, seg)


def _attn_kernel(q_ref, kc_ref, kp_ref, vc_ref, vp_ref, bias0_ref, bias_ref, o_ref):
    n_keys = (ATTN_BLOCKS + 1) * BLOCK
    low = lax.broadcasted_iota(jnp.int32, (n_keys, 2 * HEAD_DIM), 1) < HEAD_DIM
    low_q = lax.broadcasted_iota(jnp.int32, (BLOCK, 2 * HEAD_DIM), 1) < HEAD_DIM
    key0 = lax.broadcasted_iota(jnp.int32, (2 * BLOCK, 2 * HEAD_DIM), 0) == 0
    zero = jnp.zeros((), BF16)
    contract_last = (((1,), (1,)), ((), ()))
    for g in range(N_KV_HEADS):
        cols = slice(g * 2 * HEAD_DIM, (g + 1) * 2 * HEAD_DIM)
        k_all = jnp.concatenate([kp_ref[:, cols], kc_ref[:, cols]], axis=0)
        v_all = jnp.concatenate([vp_ref[:, cols], vc_ref[:, cols]], axis=0)
        k_lo_all = jnp.where(low, k_all, zero)
        k_hi_all = jnp.where(low, zero, k_all)
        pair0 = slice((2 * g) * 2 * HEAD_DIM, (2 * g + 1) * 2 * HEAD_DIM)
        pair1 = slice((2 * g + 1) * 2 * HEAD_DIM, (2 * g + 2) * 2 * HEAD_DIM)
        for blk in range(ATTN_BLOCKS):
            rows = slice(blk * BLOCK, (blk + 1) * BLOCK)
            band = slice(blk * BLOCK, (blk + 2) * BLOCK)
            b_ref = bias0_ref if blk == 0 else bias_ref
            k_lo = jnp.where(key0, zero, k_lo_all[band])
            k_hi = jnp.where(key0, zero, k_hi_all[band])
            v_band = jnp.where(key0, zero, v_all[band])
            v_ext = jnp.concatenate([v_band, jnp.ones_like(v_band)], axis=1)
            q2 = jnp.concatenate([q_ref[rows, pair0], q_ref[rows, pair1]], axis=0)
            s_lo = lax.dot_general(q2, k_lo, contract_last, preferred_element_type=F32)
            s_hi = lax.dot_general(q2, k_hi, contract_last, preferred_element_type=F32)
            sh = jnp.concatenate([s_lo, s_hi], axis=0) + b_ref[0, g]
            m = jnp.max(sh, axis=-1, keepdims=True)
            p = jnp.exp2(sh - m).astype(BF16)
            pv = jnp.dot(p, v_ext, preferred_element_type=F32)
            out = pv[:, :2 * HEAD_DIM] / pv[:, 2 * HEAD_DIM:]
            o_ref[rows, pair0] = jnp.where(low_q, out[:BLOCK], out[2 * BLOCK:3 * BLOCK]).astype(BF16)
            o_ref[rows, pair1] = jnp.where(low_q, out[BLOCK:2 * BLOCK], out[3 * BLOCK:]).astype(BF16)


def _attention(q, kd, vd, bias, batch, seq):
    t = q.shape[0]
    tq = ATTN_BLOCKS * BLOCK
    nt = seq // tq
    cur = lambda b, n: (b * nt + n, 0)
    prev = lambda b, n: ((b * nt + n) * ATTN_BLOCKS - jnp.minimum(n, 1), 0)
    bias_spec = lambda index_map: pl.BlockSpec(
        (1, N_KV_HEADS, GQA_GROUP * BLOCK, 2 * BLOCK), index_map)
    return pl.pallas_call(
        _attn_kernel,
        grid=(batch, nt),
        in_specs=[
            pl.BlockSpec((tq, ATTN_WIDTH), cur),
            pl.BlockSpec((tq, 2 * KV_WIDTH), cur),
            pl.BlockSpec((BLOCK, 2 * KV_WIDTH), prev),
            pl.BlockSpec((tq, 2 * KV_WIDTH), cur),
            pl.BlockSpec((BLOCK, 2 * KV_WIDTH), prev),
            bias_spec(lambda b, n: (jnp.minimum(n, 1), 0, 0, 0)),
            bias_spec(lambda b, n: (1, 0, 0, 0)),
        ],
        out_specs=pl.BlockSpec((tq, ATTN_WIDTH), cur),
        out_shape=jax.ShapeDtypeStruct((t, ATTN_WIDTH), BF16),
        compiler_params=_params(("arbitrary", "arbitrary")),
        name="attention",
    )(q, kd, kd, vd, vd, bias, bias)


def _attention_bias(sinks):
    heads = jnp.arange(1, N_Q_HEADS + 1, dtype=F32)
    slopes = jnp.exp2(-8.0 * heads / N_Q_HEADS)
    key = jnp.arange(2 * BLOCK)[None, :]
    dist = (jnp.arange(BLOCK)[:, None] + BLOCK) - key
    valid = (dist >= 0) & (dist < BLOCK)
    valid = jnp.stack([valid & (key >= BLOCK), valid])
    bias = -slopes[:, None, None] * dist.astype(F32)
    bias = jnp.where(valid[:, None], bias[None], -jnp.inf)
    bias = jnp.where(key == 0, sinks[None, :, None, None], bias) * LOG2_E
    order = jnp.array([[4 * g, 4 * g + 2, 4 * g + 1, 4 * g + 3] for g in range(N_KV_HEADS)])
    return bias[:, order].reshape(2, N_KV_HEADS, GQA_GROUP * BLOCK, 2 * BLOCK)


def _conv_kernel(tiles_per_seq, cur_ref, halo_ref, w_ref, b_ref, lng_ref, lnb_ref, o_ref, buf, ybuf):
    first = (pl.program_id(0) % tiles_per_seq) == 0
    groups = CONV_SEG_LEN // V7X_SUBLANES
    for s in range(CONV_SLABS):
        cols = slice(s * V7X_LANES, (s + 1) * V7X_LANES)
        for j in range(CONV_SEGS):
            for gi in range(groups):
                t0 = CONV_SEG_ROWS * j - CONV_HALO + V7X_SUBLANES * gi
                if t0 < 0:
                    h0 = CONV_HALO + t0
                    src = jnp.where(first, 0.0, halo_ref[h0:h0 + V7X_SUBLANES, cols])
                else:
                    src = cur_ref[t0:t0 + V7X_SUBLANES, cols]
                buf[s, pl.ds(CONV_SEGS * V7X_SUBLANES * gi + j, V7X_SUBLANES, stride=CONV_SEGS), :] = src

    base = CONV_HALO - (CONV_KERNEL - 1)
    out_groups = CONV_SEG_ROWS // V7X_SUBLANES

    def seg_body(it, carry):
        s = it // CONV_SEGS
        j = it % CONV_SEGS
        acc = [None] * out_groups
        for k in range(CONV_KERNEL):
            wk = w_ref[s, k]
            for gi in range(out_groups):
                i = V7X_SUBLANES * gi + base + k
                term = buf[s, pl.ds(CONV_SEGS * i + j, V7X_SUBLANES, stride=CONV_SEGS), :] * wk
                acc[gi] = term if acc[gi] is None else acc[gi] + term
        row0 = pl.multiple_of(j * CONV_SEG_ROWS, CONV_SEG_ROWS)
        ybuf[s, pl.ds(row0, CONV_SEG_ROWS), :] = jnp.concatenate(acc, axis=0)
        return carry

    lax.fori_loop(0, CONV_SLABS * CONV_SEGS, seg_body, 0, unroll=2)

    y = jnp.concatenate([ybuf[s] for s in range(CONV_SLABS)], axis=1) + b_ref[...]
    mu = jnp.mean(y, axis=-1, keepdims=True)
    yc = y - mu
    var = jnp.mean(yc * yc, axis=-1, keepdims=True)
    z = yc * lax.rsqrt(var + NORM_EPS) * lng_ref[...] + lnb_ref[...]
    o_ref[...] = (z * _sigmoid(z)).astype(BF16)


def _conv_weights(w):
    ws = w.reshape(CONV_KERNEL, CONV_SLABS, V7X_LANES).transpose(1, 0, 2)
    return jnp.broadcast_to(ws[:, :, None, :], (CONV_SLABS, CONV_KERNEL, V7X_SUBLANES, V7X_LANES))


def _conv(glu, w_slab, b, lng, lnb, seq):
    t = glu.shape[0]
    tc = TC_CONV
    ratio = tc // CONV_HALO
    fixed = lambda i: (0, 0)
    return pl.pallas_call(
        functools.partial(_conv_kernel, seq // tc),
        grid=(t // tc,),
        in_specs=[
            pl.BlockSpec((tc, CONV_WIDTH), lambda i: (i, 0)),
            pl.BlockSpec((CONV_HALO, CONV_WIDTH), lambda i: (jnp.maximum(i * ratio - 1, 0), 0)),
            pl.BlockSpec((CONV_SLABS, CONV_KERNEL, V7X_SUBLANES, V7X_LANES), lambda i: (0, 0, 0, 0)),
            pl.BlockSpec((1, CONV_WIDTH), fixed),
            pl.BlockSpec((1, CONV_WIDTH), fixed),
            pl.BlockSpec((1, CONV_WIDTH), fixed),
        ],
        out_specs=pl.BlockSpec((tc, CONV_WIDTH), lambda i: (i, 0)),
        out_shape=jax.ShapeDtypeStruct((t, CONV_WIDTH), BF16),
        scratch_shapes=[pltpu.VMEM((CONV_SLABS, CONV_SEGS * CONV_SEG_LEN, V7X_LANES), F32),
                        pltpu.VMEM((CONV_SLABS, tc, V7X_LANES), F32)],
        compiler_params=_params(("arbitrary",)),
        name="conv",
    )(glu, glu, w_slab, b, lng, lnb)


def _cast_weight_once(w_ref, w_bf16):
    @pl.when(pl.program_id(0) == 0)
    def _():
        w_bf16[...] = w_ref[...].astype(BF16)


def _out_proj_residual(attn_ref, conv_ref, x_ref, w_bf16):
    y = (jnp.dot(attn_ref[...], w_bf16[:ATTN_WIDTH], preferred_element_type=F32)
         + jnp.dot(conv_ref[...], w_bf16[ATTN_WIDTH:], preferred_element_type=F32))
    return x_ref[...] + y


def _mixer_out_router_kernel(attn_ref, conv_ref, x_ref, w_ref, g_ref, wr_ref,
                             o_ref, route_ref, count_ref, w_bf16, tri, carry):
    i = pl.program_id(0)
    tm = x_ref.shape[0]

    @pl.when(i == 0)
    def _():
        carry[...] = jnp.zeros_like(carry)
        earlier = (lax.broadcasted_iota(jnp.int32, (tm, tm), 0)
                   < lax.broadcasted_iota(jnp.int32, (tm, tm), 1))
        tri[...] = jnp.where(earlier, 1.0, 0.0).astype(BF16)

    _cast_weight_once(w_ref, w_bf16)
    x = _out_proj_residual(attn_ref, conv_ref, x_ref, w_bf16)
    o_ref[...] = x

    ms = jnp.mean(x * x, axis=-1, keepdims=True)
    h = x * lax.rsqrt(ms + NORM_EPS) * g_ref[...]
    h_hi = h.astype(BF16)
    h_lo = (h - h_hi.astype(F32)).astype(BF16)
    both = jnp.dot(jnp.concatenate([h_hi, h_lo], axis=1), wr_ref[...],
                   preferred_element_type=F32)
    logits = both[:, :V7X_LANES] + both[:, V7X_LANES:]

    lt = jnp.transpose(logits)[:N_EXPERTS]
    sub = lax.broadcasted_iota(jnp.int32, (N_EXPERTS, tm), 0)
    sub_f = sub.astype(F32)
    neg_inf = -jnp.inf
    none = float(N_EXPERTS)
    m1 = jnp.max(lt, axis=0, keepdims=True)
    i1 = jnp.min(jnp.where(lt == m1, sub_f, none), axis=0, keepdims=True)
    oh1 = sub_f == i1
    lt2 = jnp.where(oh1, neg_inf, lt)
    m2 = jnp.max(lt2, axis=0, keepdims=True)
    i2 = jnp.min(jnp.where(lt2 == m2, sub_f, none), axis=0, keepdims=True)
    oh2 = sub_f == i2
    e = jnp.exp(m2 - m1)
    w1 = 1.0 / (1.0 + e)
    w2 = e / (1.0 + e)

    member = jnp.where(oh1 | oh2, 1.0, 0.0)
    before = (jnp.dot(member.astype(BF16), tri[...], preferred_element_type=F32)
              + carry[:, 0:1])
    rank1 = jnp.sum(jnp.where(oh1, before, 0.0), axis=0, keepdims=True)
    rank2 = jnp.sum(jnp.where(oh2, before, 0.0), axis=0, keepdims=True)
    carry[...] = carry[...] + jnp.sum(member, axis=1, keepdims=True)

    route = jnp.where(sub == 0, i1, 0.0)
    route = jnp.where(sub == 1, i2, route)
    route = jnp.where(sub == 2, w1, route)
    route = jnp.where(sub == 3, w2, route)
    route = jnp.where(sub == 4, rank1, route)
    route = jnp.where(sub == 5, rank2, route)
    route_ref[0] = route
    count_ref[...] = carry[...]


def _mixer_out_router(attn, conv, x, w_out, layer, g, wr_split):
    t = x.shape[0]
    tm = TM_PROJ
    row = lambda i: (i, 0)
    fixed = lambda i: (0, 0)
    in_specs = [
        pl.BlockSpec((tm, ATTN_WIDTH), row),
        pl.BlockSpec((tm, CONV_WIDTH), row),
        pl.BlockSpec((tm, D_MODEL), row),
        pl.BlockSpec((None, D_MODEL, D_MODEL), lambda i: (layer, 0, 0)),
    ]
    w_scratch = pltpu.VMEM((D_MODEL, D_MODEL), BF16)
    return pl.pallas_call(
        _mixer_out_router_kernel,
        grid=(t // tm,),
        in_specs=in_specs + [
            pl.BlockSpec((1, D_MODEL), fixed),
            pl.BlockSpec((2 * D_MODEL, 2 * V7X_LANES), fixed),
        ],
        out_specs=[
            pl.BlockSpec((tm, D_MODEL), row),
            pl.BlockSpec((1, N_EXPERTS, tm), lambda i: (i, 0, 0)),
            pl.BlockSpec((N_EXPERTS, V7X_LANES), fixed),
        ],
        out_shape=[
            jax.ShapeDtypeStruct((t, D_MODEL), F32),
            jax.ShapeDtypeStruct((t // tm, N_EXPERTS, tm), F32),
            jax.ShapeDtypeStruct((N_EXPERTS, V7X_LANES), F32),
        ],
        scratch_shapes=[w_scratch, pltpu.VMEM((tm, tm), BF16),
                        pltpu.VMEM((N_EXPERTS, V7X_LANES), F32)],
        compiler_params=_params(("arbitrary",)),
        name="mixer_out_router",
    )(attn, conv, x, w_out, g, wr_split)


def _ffn_chunk_copies(w_hbm, lead, j, tf, stage, sem, slot):
    cols = pl.ds(j * tf, tf)
    every = slice(None)
    index = (lead + (every, cols), lead + (every, cols), lead + (cols, every))
    return [pltpu.make_async_copy(w_hbm[m].at[index[m]], stage[m].at[slot], sem.at[m, slot])
            for m in range(3)]


def _prefetch_weights(w_hbm, lead, tf, stage, sem):
    for j in range(2):
        for c in _ffn_chunk_copies(w_hbm, lead, j, tf, stage, sem, j):
            c.start()


def _swiglu_chunk(h, res, j):
    gate = jnp.dot(h, res[0][j], preferred_element_type=F32)
    up = jnp.dot(h, res[1][j], preferred_element_type=F32)
    a = (gate * _sigmoid(gate) * up).astype(BF16)
    return jnp.dot(a, res[2][j], preferred_element_type=F32)


def _swiglu_tile(load_weights, prepare, emit, w_hbm, lead, res, stage, sem, hbf, acc, nj, tf,
                 half=None):
    @pl.when(load_weights)
    def _():
        h, ctx = prepare()
        hbf[...] = h
        for j in range(nj):
            slot = j % 2
            for c in _ffn_chunk_copies(w_hbm, lead, j, tf, stage, sem, slot):
                c.wait()
            for m in range(3):
                res[m][j] = stage[m][slot].astype(BF16)
            if j + 2 < nj:
                for c in _ffn_chunk_copies(w_hbm, lead, j + 2, tf, stage, sem, slot):
                    c.start()
            yj = _swiglu_chunk(hbf[...], res, j)
            if j == 0:
                acc[...] = yj
            else:
                acc[...] += yj
        emit(acc[...], ctx)

    def straight_line(make_h, finish):
        h, ctx = make_h()
        y = _swiglu_chunk(h, res, 0)
        for j in range(1, nj):
            y = y + _swiglu_chunk(h, res, j)
        finish(y, ctx)

    resident = jnp.logical_not(load_weights)
    if half is None:
        pl.when(resident)(lambda: straight_line(prepare, emit))
    else:
        is_half, prepare_half, emit_half = half
        pl.when(resident & jnp.logical_not(is_half))(lambda: straight_line(prepare, emit))
        pl.when(resident & is_half)(lambda: straight_line(prepare_half, emit_half))


def _ffn_scratch(tm, nj, tf):
    assert nj >= 2
    return [
        pltpu.VMEM((nj, D_MODEL, tf), BF16),
        pltpu.VMEM((nj, D_MODEL, tf), BF16),
        pltpu.VMEM((nj, tf, D_MODEL), BF16),
        pltpu.VMEM((2, D_MODEL, tf), F32),
        pltpu.VMEM((2, D_MODEL, tf), F32),
        pltpu.VMEM((2, tf, D_MODEL), F32),
        pltpu.SemaphoreType.DMA((3, 2)),
        pltpu.VMEM((tm, D_MODEL), BF16),
        pltpu.VMEM((tm, D_MODEL), F32),
    ]


def _dense_ffn_kernel(ffn_layer, nj, tf, attn_ref, conv_ref, x_ref, w_out_ref, g_ref, wg, wu, wd,
                      o_ref, w_out_bf16, rg, ru, rd, sg, su, sd, sem, hbf, acc):
    first = pl.program_id(0) == 0

    @pl.when(first)
    def _():
        _prefetch_weights((wg, wu, wd), (ffn_layer,), tf, (sg, su, sd), sem)

    _cast_weight_once(w_out_ref, w_out_bf16)

    def prepare():
        x_mid = _out_proj_residual(attn_ref, conv_ref, x_ref, w_out_bf16)
        return _rms_norm_bf16(x_mid, g_ref[...]), x_mid

    def emit(y, x_mid):
        o_ref[...] = x_mid + y

    _swiglu_tile(first, prepare, emit, (wg, wu, wd), (ffn_layer,), (rg, ru, rd),
                 (sg, su, sd), sem, hbf, acc, nj, tf)


def _mixer_out_dense_ffn(attn, conv, x, w_out, layer, g, wg, wu, wd, ffn_layer):
    t = x.shape[0]
    tm = TM_FFN
    tf = TF_DENSE
    nj = wg.shape[2] // tf
    row = lambda i: (i, 0)
    hbm = pl.BlockSpec(memory_space=pl.ANY)
    return pl.pallas_call(
        functools.partial(_dense_ffn_kernel, ffn_layer, nj, tf),
        grid=(t // tm,),
        in_specs=[
            pl.BlockSpec((tm, ATTN_WIDTH), row),
            pl.BlockSpec((tm, CONV_WIDTH), row),
            pl.BlockSpec((tm, D_MODEL), row),
            pl.BlockSpec((None, D_MODEL, D_MODEL), lambda i: (layer, 0, 0)),
            pl.BlockSpec((1, D_MODEL), lambda i: (0, 0)),
            hbm, hbm, hbm,
        ],
        out_specs=pl.BlockSpec((tm, D_MODEL), row),
        out_shape=jax.ShapeDtypeStruct((t, D_MODEL), F32),
        scratch_shapes=[pltpu.VMEM((D_MODEL, D_MODEL), BF16)] + _ffn_scratch(tm, nj, tf),
        compiler_params=_params(("arbitrary",)),
        name="dense_ffn",
    )(attn, conv, x, w_out, g, wg, wu, wd)


def _token_rows(ref, row):
    return ref.at[pl.ds(pl.multiple_of(row * ROW_CHUNKS, ROW_CHUNKS), ROW_CHUNKS), :]


def _dispatch_kernel(fill_ref, pos_ref, x_ref, g_ref, xs_ref, buf, sem, fill_sem):
    tm = x_ref.shape[0]

    @pl.when(pl.program_id(0) == 0)
    def _():
        buf[...] = jnp.zeros_like(buf)
        tile_len = TM_MOE * ROW_CHUNKS

        def fill(n):
            start = pl.multiple_of(fill_ref[n + 1] * tile_len, tile_len)
            return pltpu.make_async_copy(buf.at[pl.ds(0, tile_len), :],
                                         xs_ref.at[pl.ds(start, tile_len), :], fill_sem)

        def start_fill(n, carry):
            fill(n).start()
            return carry

        def wait_fill(n, carry):
            fill(n).wait()
            return carry

        lax.fori_loop(0, fill_ref[0], start_fill, 0)
        lax.fori_loop(0, fill_ref[0], wait_fill, 0)

    x = x_ref[...]
    ms = jnp.mean(x * x, axis=-1, keepdims=True)
    h = x * lax.rsqrt(ms + NORM_EPS) * g_ref[...]
    for c in range(ROW_CHUNKS):
        buf[pl.ds(c, tm, stride=ROW_CHUNKS), :] = h[:, c * V7X_LANES:(c + 1) * V7X_LANES]

    def issue(r, carry):
        src = _token_rows(buf, r)
        for k in range(2):
            dst = _token_rows(xs_ref, pos_ref[0, 0, 2 * r + k])
            pltpu.make_async_copy(src, dst, sem).start(priority=k)
        return carry

    lax.fori_loop(0, tm, issue, 0)
    for k in range(2):
        pltpu.make_async_copy(buf, xs_ref.at[pl.ds(0, tm * ROW_CHUNKS), :], sem).wait()


def _dispatch(fill_tiles, pos_tiles, x, g, n_rows):
    t = x.shape[0]
    tm = TM_TOK
    assert tm >= TM_MOE
    grid_spec = pltpu.PrefetchScalarGridSpec(
        num_scalar_prefetch=1,
        grid=(t // tm,),
        in_specs=[
            pl.BlockSpec((1, 1, 2 * tm), lambda i, fill: (i, 0, 0), memory_space=pltpu.SMEM),
            pl.BlockSpec((tm, D_MODEL), lambda i, fill: (i, 0)),
            pl.BlockSpec((1, D_MODEL), lambda i, fill: (0, 0)),
        ],
        out_specs=pl.BlockSpec(memory_space=pl.ANY),
        scratch_shapes=[pltpu.VMEM((tm * ROW_CHUNKS, V7X_LANES), F32),
                        pltpu.SemaphoreType.DMA(()),
                        pltpu.SemaphoreType.DMA(())],
    )
    return pl.pallas_call(
        _dispatch_kernel,
        grid_spec=grid_spec,
        out_shape=jax.ShapeDtypeStruct((n_rows * ROW_CHUNKS, V7X_LANES), F32),
        compiler_params=_params(("arbitrary",)),
        name="moe_dispatch",
    )(fill_tiles, pos_tiles, x, g)


def _moe_ffn_kernel(layer, nj, tf, te_ref, tr_ref, tl_ref, xs_ref, wg, wu, wd, ys_ref,
                    rg, ru, rd, sg, su, sd, sem, hbf, acc):
    i = pl.program_id(0)
    tm = hbf.shape[0]
    rows = tr_ref[i]
    w_hbm = (wg, wu, wd)
    stage = (sg, su, sd)

    loads = tl_ref[i] != 0
    next_loads = tl_ref[i + 1] != 0

    @pl.when((i == 0) & loads)
    def _():
        _prefetch_weights(w_hbm, (layer, te_ref[0]), tf, stage, sem)

    @pl.when(next_loads & jnp.logical_not(loads))
    def _():
        _prefetch_weights(w_hbm, (layer, te_ref[i + 1]), tf, stage, sem)

    @pl.when(rows > 0)
    def _():
        def load_rows(n):
            x = jnp.concatenate(
                [xs_ref[pl.ds(c, n, stride=ROW_CHUNKS), :] for c in range(ROW_CHUNKS)], axis=1)
            return x.astype(BF16), None

        def store_rows(y, n):
            for c in range(ROW_CHUNKS):
                ys_ref[pl.ds(c, n, stride=ROW_CHUNKS), :] = y[:, c * V7X_LANES:(c + 1) * V7X_LANES]

        half_rows = tm // 2

        def emit_half(y, _):
            store_rows(y, half_rows)
            rest = pl.ds(half_rows * ROW_CHUNKS, (tm - half_rows) * ROW_CHUNKS)
            ys_ref[rest, :] = jnp.zeros(((tm - half_rows) * ROW_CHUNKS, V7X_LANES), F32)

        _swiglu_tile(loads, lambda: load_rows(tm), lambda y, _: store_rows(y, tm),
                     w_hbm, (layer, te_ref[i]), (rg, ru, rd), stage, sem, hbf, acc, nj, tf,
                     half=(rows <= half_rows, lambda: load_rows(half_rows), emit_half))

    @pl.when(rows == 0)
    def _():
        ys_ref[...] = jnp.zeros_like(ys_ref)

    @pl.when(next_loads & loads)
    def _():
        _prefetch_weights(w_hbm, (layer, te_ref[i + 1]), tf, stage, sem)


def _moe_ffn(tile_expert, tile_rows, tile_load, xs, wg, wu, wd, layer):
    n_tiles = tile_rows.shape[0]
    assert tile_expert.shape[0] == n_tiles + 1 and tile_load.shape[0] == n_tiles + 1
    tm = TM_MOE
    tf = TF_MOE
    nj = wg.shape[3] // tf
    hbm = pl.BlockSpec(memory_space=pl.ANY)
    grid_spec = pltpu.PrefetchScalarGridSpec(
        num_scalar_prefetch=3,
        grid=(n_tiles,),
        in_specs=[
            pl.BlockSpec((tm * ROW_CHUNKS, V7X_LANES), lambda i, te, tr, tl: (i, 0)),
            hbm, hbm, hbm,
        ],
        out_specs=pl.BlockSpec((tm * ROW_CHUNKS, V7X_LANES), lambda i, te, tr, tl: (i, 0)),
        scratch_shapes=_ffn_scratch(tm, nj, tf),
    )
    return pl.pallas_call(
        functools.partial(_moe_ffn_kernel, layer, nj, tf),
        grid_spec=grid_spec,
        out_shape=jax.ShapeDtypeStruct(xs.shape, F32),
        compiler_params=_params(("arbitrary",)),
        name="moe_ffn",
    )(tile_expert, tile_rows, tile_load, xs, wg, wu, wd)


def _combine_kernel(pos_ref, x_ref, w_ref, ys_ref, o_ref, y1buf, y2buf, sem):
    tm = x_ref.shape[0]
    half = tm // 2

    def issue(h):
        def body(r, carry):
            for k, ybuf in ((0, y1buf), (1, y2buf)):
                src = _token_rows(ys_ref, pos_ref[0, 0, 2 * r + k])
                pltpu.make_async_copy(src, _token_rows(ybuf, r), sem.at[h]).start(priority=k)
            return carry
        lax.fori_loop(h * half, (h + 1) * half, body, 0, unroll=4)

    issue(0)
    issue(1)
    for h in range(2):
        buf_rows = pl.ds(h * half * ROW_CHUNKS, half * ROW_CHUNKS)
        for ybuf in (y1buf, y2buf):
            pltpu.make_async_copy(ys_ref.at[pl.ds(0, half * ROW_CHUNKS), :], ybuf.at[buf_rows, :],
                                  sem.at[h]).wait()
        tok = slice(h * half, (h + 1) * half)
        w1 = w_ref[tok, 0:1]
        w2 = w_ref[tok, 1:2]
        for c in range(ROW_CHUNKS):
            cols = slice(c * V7X_LANES, (c + 1) * V7X_LANES)
            y1 = y1buf[pl.ds(h * half * ROW_CHUNKS + c, half, stride=ROW_CHUNKS), :]
            y2 = y2buf[pl.ds(h * half * ROW_CHUNKS + c, half, stride=ROW_CHUNKS), :]
            o_ref[tok, cols] = x_ref[tok, cols] + (w1 * y1 + w2 * y2)


def _combine(pos_tiles, x, gate_w, ys):
    t = x.shape[0]
    tm = TM_TOK
    return pl.pallas_call(
        _combine_kernel,
        grid=(t // tm,),
        in_specs=[
            pl.BlockSpec((1, 1, 2 * tm), lambda i: (i, 0, 0), memory_space=pltpu.SMEM),
            pl.BlockSpec((tm, D_MODEL), lambda i: (i, 0)),
            pl.BlockSpec((tm, 2), lambda i: (i, 0)),
            pl.BlockSpec(memory_space=pl.ANY),
        ],
        out_specs=pl.BlockSpec((tm, D_MODEL), lambda i: (i, 0)),
        out_shape=jax.ShapeDtypeStruct((t, D_MODEL), F32),
        scratch_shapes=[pltpu.VMEM((tm * ROW_CHUNKS, V7X_LANES), F32),
                        pltpu.VMEM((tm * ROW_CHUNKS, V7X_LANES), F32),
                        pltpu.SemaphoreType.DMA((2,))],
        compiler_params=_params(("arbitrary",)),
        name="moe_combine",
    )(pos_tiles, x, gate_w, ys)


def _moe_layer(x, route, counts, g, wg, wu, wd, layer):
    t = x.shape[0]
    tm = TM_MOE
    n_tiles = (2 * t) // tm + N_EXPERTS
    per_token = lambda rows: rows.transpose(0, 2, 1).reshape(t, 2)
    expert = per_token(route[:, 0:2]).astype(jnp.int32)
    gate_w = per_token(route[:, 2:4])
    rank = per_token(route[:, 4:6]).astype(jnp.int32)
    count = counts[:, 0].astype(jnp.int32)
    padded = ((count + tm - 1) // tm) * tm
    gend = jnp.cumsum(padded)
    gstart = gend - padded
    ids = jnp.arange(N_EXPERTS, dtype=jnp.int32)
    pos = rank + jnp.sum(jnp.where(expert[..., None] == ids, gstart, 0), axis=-1)
    tile_start = jnp.arange(n_tiles, dtype=jnp.int32) * tm
    last_expert = jnp.max(jnp.where(padded > 0, ids, 0))
    tile_expert = jnp.minimum(
        jnp.sum((tile_start[:, None] >= gend[None, :]).astype(jnp.int32), axis=1), last_expert)
    used_end = (gstart + count)[tile_expert]
    tile_rows = jnp.clip(used_end - tile_start, 0, tm)
    prev_expert = jnp.concatenate([jnp.full((1,), -1, jnp.int32), tile_expert[:-1]])
    tile_load = ((tile_rows > 0) & (tile_expert != prev_expert)).astype(jnp.int32)
    pos_tiles = pos.reshape(t // TM_TOK, 1, 2 * TM_TOK)

    max_fill = n_tiles - (2 * t - N_EXPERTS * (tm - 1)) // tm
    partial = tile_rows < tm
    fill_idx = jnp.nonzero(partial, size=max_fill, fill_value=0)[0].astype(jnp.int32)
    fill_tiles = jnp.concatenate([jnp.sum(partial, dtype=jnp.int32)[None], fill_idx])

    xs = _dispatch(fill_tiles, pos_tiles, x, g, n_tiles * tm)
    one_more = lambda v, last: jnp.concatenate([v, jnp.full((1,), last, jnp.int32)])
    ys = _moe_ffn(one_more(tile_expert, 0), tile_rows, one_more(tile_load, 0), xs,
                  wg, wu, wd, layer)
    return _combine(pos_tiles, x, gate_w, ys)


def kernel(x, attn_norm_g, w_in, q_norm_g, k_norm_g, sinks, conv_w, conv_b, conv_ln_g,
           conv_ln_b, w_out, ffn_norm_g, dense_w_gate, dense_w_up, dense_w_down,
           w_router, moe_w_gate, moe_w_up, moe_w_down):
    batch, seq, d = x.shape
    depth = w_in.shape[0]
    t = batch * seq
    assert d == D_MODEL and seq % TC_CONV == 0 and t % TM_PROJ == 0 and t % TM_TOK == 0
    assert seq % (ATTN_BLOCKS * BLOCK) == 0
    assert w_in.shape[2] == IN_PROJ_WIDTH and conv_w.shape[1] == CONV_KERNEL
    assert dense_w_gate.shape[2] % TF_DENSE == 0 and moe_w_gate.shape[3] % TF_MOE == 0

    xt = x.reshape(t, d)
    head_id = jnp.arange(SEG_WIDTH) // HEAD_DIM
    seg = (head_id[:, None] == head_id[None, :]).astype(BF16)
    scale = HEAD_DIM ** -0.5 * LOG2_E

    for layer in range(depth):
        gain = jnp.concatenate([jnp.tile(q_norm_g[layer], N_Q_HEADS) * scale,
                                jnp.tile(k_norm_g[layer], N_KV_HEADS)])[None, :]
        q, kd, vd, glu = _mixer_in(xt, attn_norm_g[layer][None, :], w_in, layer, gain, seg)
        attn = _attention(q, kd, vd, _attention_bias(sinks[layer]), batch, seq)
        conv = _conv(glu, _conv_weights(conv_w[layer]), conv_b[layer][None, :],
                     conv_ln_g[layer][None, :],
                     conv_ln_b[layer][None, :], seq)
        ffn_g = ffn_norm_g[layer][None, :]
        i = layer // 2
        if layer % 2 == 0:
            xt = _mixer_out_dense_ffn(attn, conv, xt, w_out, layer, ffn_g,
                                      dense_w_gate, dense_w_up, dense_w_down, i)
        else:
            wr = jnp.zeros((D_MODEL, V7X_LANES), F32).at[:, :N_EXPERTS].set(w_router[i])
            wr_hi = wr.astype(BF16)
            wr_lo = (wr - wr_hi.astype(F32)).astype(BF16)
            wr_split = jnp.concatenate(
                [jnp.concatenate([wr_hi, wr_lo], axis=1),
                 jnp.concatenate([wr_hi, jnp.zeros_like(wr_lo)], axis=1)], axis=0)
            xm, route, counts = _mixer_out_router(attn, conv, xt, w_out, layer, ffn_g, wr_split)
            xt = _moe_layer(xm, route, counts, ffn_g, moe_w_gate, moe_w_up, moe_w_down, i)
    return xt.reshape(batch, seq, d)
```

```python
import functools

import jax
import jax.numpy as jnp
from jax import lax
from jax.experimental import pallas as pl
from jax.experimental.pallas import tpu as pltpu

F32 = jnp.float32
BF16 = jnp.bfloat16

D_MODEL = 1024
HEAD_DIM = 64
N_Q_HEADS = 8
N_KV_HEADS = 2
GQA_GROUP = N_Q_HEADS // N_KV_HEADS
ATTN_WIDTH = N_Q_HEADS * HEAD_DIM
KV_WIDTH = N_KV_HEADS * HEAD_DIM
QK_WIDTH = ATTN_WIDTH + KV_WIDTH
SEG_WIDTH = 256
BLOCK = 128
CONV_WIDTH = D_MODEL - ATTN_WIDTH
CONV_KERNEL = 31
IN_PROJ_WIDTH = ATTN_WIDTH + 2 * KV_WIDTH + 2 * CONV_WIDTH
N_EXPERTS = 8
NORM_EPS = 1e-6
LOG2_E = 1.4426950408889634

V7X_LANES = 128
V7X_SUBLANES = 8
V7X_VMEM_LIMIT_BYTES = 56 * 1024 * 1024
ROW_CHUNKS = D_MODEL // V7X_LANES

TM_PROJ = 1024
MIXER_IN_PARTS = 2
ATTN_BLOCKS = 16
TC_CONV = 256
CONV_HALO = 32
CONV_SLABS = CONV_WIDTH // V7X_LANES
CONV_SEGS = 4
CONV_SEG_ROWS = TC_CONV // CONV_SEGS
CONV_SEG_LEN = CONV_SEG_ROWS + CONV_HALO
TM_FFN = 512
TF_DENSE = 256
TM_MOE = 512
TF_MOE = 512
TM_TOK = 512


def _params(sem, vmem=V7X_VMEM_LIMIT_BYTES):
    return pltpu.CompilerParams(dimension_semantics=sem, vmem_limit_bytes=vmem)


def _sigmoid(x):
    return 1.0 / (1.0 + jnp.exp(-x))


def _rms_norm_bf16(x, g):
    ms = jnp.mean(x * x, axis=-1, keepdims=True)
    return (x * lax.rsqrt(ms + NORM_EPS) * g).astype(BF16)


def _mixer_in_kernel(x_ref, g_ref, w_ref, gain_ref, seg_ref, q_ref, k_ref, v_ref, glu_ref, w_bf16):
    @pl.when(pl.program_id(0) == 0)
    def _():
        w_bf16[...] = w_ref[...].astype(BF16)

    part = x_ref.shape[0] // MIXER_IN_PARTS
    seg = seg_ref[...]
    low = lax.broadcasted_iota(jnp.int32, (part, KV_WIDTH), 1) < HEAD_DIM
    for p in range(MIXER_IN_PARTS):
        rows = slice(p * part, (p + 1) * part)
        h = _rms_norm_bf16(x_ref[rows], g_ref[...])
        proj = jnp.dot(h, w_bf16[...], preferred_element_type=F32)
        qk = proj[:, :QK_WIDTH]
        sq = qk * qk
        hi = sq.astype(BF16)
        lo = (sq - hi.astype(F32)).astype(BF16)
        parts = []
        for c0 in range(0, QK_WIDTH, SEG_WIDTH):
            width = min(SEG_WIDTH, QK_WIDTH - c0)
            blk = slice(c0, c0 + width)
            parts.append(jnp.dot(hi[:, blk], seg[:width, :width], preferred_element_type=F32)
                         + jnp.dot(lo[:, blk], seg[:width, :width], preferred_element_type=F32))
        ss = jnp.concatenate(parts, axis=1)
        qkn = qk * lax.rsqrt(ss * (1.0 / HEAD_DIM) + NORM_EPS) * gain_ref[...]
        q_ref[rows] = qkn[:, :ATTN_WIDTH].astype(BF16)
        for src, dst in ((qkn[:, ATTN_WIDTH:QK_WIDTH], k_ref),
                         (proj[:, QK_WIDTH:QK_WIDTH + KV_WIDTH], v_ref)):
            rot = pltpu.roll(src, HEAD_DIM, axis=1)
            dst[rows, :KV_WIDTH] = jnp.where(low, src, rot).astype(BF16)
            dst[rows, KV_WIDTH:] = jnp.where(low, rot, src).astype(BF16)
        o_c = QK_WIDTH + KV_WIDTH
        u = proj[:, o_c:o_c + CONV_WIDTH]
        gate = proj[:, o_c + CONV_WIDTH:]
        glu_ref[rows] = u * _sigmoid(gate)


def _mixer_in(x, g, w_in, layer, gain, seg):
    t = x.shape[0]
    tm = TM_PROJ
    row = lambda i: (i, 0)
    fixed = lambda i: (0, 0)
    return pl.pallas_call(
        _mixer_in_kernel,
        grid=(t // tm,),
        in_specs=[
            pl.BlockSpec((tm, D_MODEL), row),
            pl.BlockSpec((1, D_MODEL), fixed),
            pl.BlockSpec((None, D_MODEL, IN_PROJ_WIDTH), lambda i: (layer, 0, 0)),
            pl.BlockSpec((1, QK_WIDTH), fixed),
            pl.BlockSpec((SEG_WIDTH, SEG_WIDTH), fixed),
        ],
        out_specs=[
            pl.BlockSpec((tm, ATTN_WIDTH), row),
            pl.BlockSpec((tm, 2 * KV_WIDTH), row),
            pl.BlockSpec((tm, 2 * KV_WIDTH), row),
            pl.BlockSpec((tm, CONV_WIDTH), row),
        ],
        out_shape=[
            jax.ShapeDtypeStruct((t, ATTN_WIDTH), BF16),
            jax.ShapeDtypeStruct((t, 2 * KV_WIDTH), BF16),
            jax.ShapeDtypeStruct((t, 2 * KV_WIDTH), BF16),
            jax.ShapeDtypeStruct((t, CONV_WIDTH), F32),
        ],
        scratch_shapes=[pltpu.VMEM((D_MODEL, IN_PROJ_WIDTH), BF16)],
        compiler_params=_params(("arbitrary",)),
        name="mixer_in",
    )(x, g, w_in, gain, seg)


def _attn_kernel(q_ref, kc_ref, kp_ref, vc_ref, vp_ref, bias0_ref, bias_ref, o_ref):
    n_keys = (ATTN_BLOCKS + 1) * BLOCK
    low = lax.broadcasted_iota(jnp.int32, (n_keys, 2 * HEAD_DIM), 1) < HEAD_DIM
    low_q = lax.broadcasted_iota(jnp.int32, (BLOCK, 2 * HEAD_DIM), 1) < HEAD_DIM
    key0 = lax.broadcasted_iota(jnp.int32, (2 * BLOCK, 2 * HEAD_DIM), 0) == 0
    zero = jnp.zeros((), BF16)
    contract_last = (((1,), (1,)), ((), ()))
    for g in range(N_KV_HEADS):
        cols = slice(g * 2 * HEAD_DIM, (g + 1) * 2 * HEAD_DIM)
        k_all = jnp.concatenate([kp_ref[:, cols], kc_ref[:, cols]], axis=0)
        v_all = jnp.concatenate([vp_ref[:, cols], vc_ref[:, cols]], axis=0)
        k_lo_all = jnp.where(low, k_all, zero)
        k_hi_all = jnp.where(low, zero, k_all)
        pair0 = slice((2 * g) * 2 * HEAD_DIM, (2 * g + 1) * 2 * HEAD_DIM)
        pair1 = slice((2 * g + 1) * 2 * HEAD_DIM, (2 * g + 2) * 2 * HEAD_DIM)
        for blk in range(ATTN_BLOCKS):
            rows = slice(blk * BLOCK, (blk + 1) * BLOCK)
            band = slice(blk * BLOCK, (blk + 2) * BLOCK)
            b_ref = bias0_ref if blk == 0 else bias_ref
            k_lo = jnp.where(key0, zero, k_lo_all[band])
            k_hi = jnp.where(key0, zero, k_hi_all[band])
            v_band = jnp.where(key0, zero, v_all[band])
            v_ext = jnp.concatenate([v_band, jnp.ones_like(v_band)], axis=1)
            q2 = jnp.concatenate([q_ref[rows, pair0], q_ref[rows, pair1]], axis=0)
            s_lo = lax.dot_general(q2, k_lo, contract_last, preferred_element_type=F32)
            s_hi = lax.dot_general(q2, k_hi, contract_last, preferred_element_type=F32)
            sh = jnp.concatenate([s_lo, s_hi], axis=0) + b_ref[0, g]
            m = jnp.max(sh, axis=-1, keepdims=True)
            p = jnp.exp2(sh - m).astype(BF16)
            pv = jnp.dot(p, v_ext, preferred_element_type=F32)
            out = pv[:, :2 * HEAD_DIM] / pv[:, 2 * HEAD_DIM:]
            o_ref[rows, pair0] = jnp.where(low_q, out[:BLOCK], out[2 * BLOCK:3 * BLOCK]).astype(BF16)
            o_ref[rows, pair1] = jnp.where(low_q, out[BLOCK:2 * BLOCK], out[3 * BLOCK:]).astype(BF16)


def _attention(q, kd, vd, bias, batch, seq):
    t = q.shape[0]
    tq = ATTN_BLOCKS * BLOCK
    nt = seq // tq
    cur = lambda b, n: (b * nt + n, 0)
    prev = lambda b, n: ((b * nt + n) * ATTN_BLOCKS - jnp.minimum(n, 1), 0)
    bias_spec = lambda index_map: pl.BlockSpec(
        (1, N_KV_HEADS, GQA_GROUP * BLOCK, 2 * BLOCK), index_map)
    return pl.pallas_call(
        _attn_kernel,
        grid=(batch, nt),
        in_specs=[
            pl.BlockSpec((tq, ATTN_WIDTH), cur),
            pl.BlockSpec((tq, 2 * KV_WIDTH), cur),
            pl.BlockSpec((BLOCK, 2 * KV_WIDTH), prev),
            pl.BlockSpec((tq, 2 * KV_WIDTH), cur),
            pl.BlockSpec((BLOCK, 2 * KV_WIDTH), prev),
            bias_spec(lambda b, n: (jnp.minimum(n, 1), 0, 0, 0)),
            bias_spec(lambda b, n: (1, 0, 0, 0)),
        ],
        out_specs=pl.BlockSpec((tq, ATTN_WIDTH), cur),
        out_shape=jax.ShapeDtypeStruct((t, ATTN_WIDTH), BF16),
        compiler_params=_params(("arbitrary", "arbitrary")),
        name="attention",
    )(q, kd, kd, vd, vd, bias, bias)


def _attention_bias(sinks):
    heads = jnp.arange(1, N_Q_HEADS + 1, dtype=F32)
    slopes = jnp.exp2(-8.0 * heads / N_Q_HEADS)
    key = jnp.arange(2 * BLOCK)[None, :]
    dist = (jnp.arange(BLOCK)[:, None] + BLOCK) - key
    valid = (dist >= 0) & (dist < BLOCK)
    valid = jnp.stack([valid & (key >= BLOCK), valid])
    bias = -slopes[:, None, None] * dist.astype(F32)
    bias = jnp.where(valid[:, None], bias[None], -jnp.inf)
    bias = jnp.where(key == 0, sinks[None, :, None, None], bias) * LOG2_E
    order = jnp.array([[4 * g, 4 * g + 2, 4 * g + 1, 4 * g + 3] for g in range(N_KV_HEADS)])
    return bias[:, order].reshape(2, N_KV_HEADS, GQA_GROUP * BLOCK, 2 * BLOCK)


def _conv_kernel(tiles_per_seq, cur_ref, halo_ref, w_ref, b_ref, lng_ref, lnb_ref, o_ref, buf, ybuf):
    first = (pl.program_id(0) % tiles_per_seq) == 0
    groups = CONV_SEG_LEN // V7X_SUBLANES
    for s in range(CONV_SLABS):
        cols = slice(s * V7X_LANES, (s + 1) * V7X_LANES)
        for j in range(CONV_SEGS):
            for gi in range(groups):
                t0 = CONV_SEG_ROWS * j - CONV_HALO + V7X_SUBLANES * gi
                if t0 < 0:
                    h0 = CONV_HALO + t0
                    src = jnp.where(first, 0.0, halo_ref[h0:h0 + V7X_SUBLANES, cols])
                else:
                    src = cur_ref[t0:t0 + V7X_SUBLANES, cols]
                buf[s, pl.ds(CONV_SEGS * V7X_SUBLANES * gi + j, V7X_SUBLANES, stride=CONV_SEGS), :] = src

    base = CONV_HALO - (CONV_KERNEL - 1)
    out_groups = CONV_SEG_ROWS // V7X_SUBLANES

    def seg_body(it, carry):
        s = it // CONV_SEGS
        j = it % CONV_SEGS
        acc = [None] * out_groups
        for k in range(CONV_KERNEL):
            wk = w_ref[s, k]
            for gi in range(out_groups):
                i = V7X_SUBLANES * gi + base + k
                term = buf[s, pl.ds(CONV_SEGS * i + j, V7X_SUBLANES, stride=CONV_SEGS), :] * wk
                acc[gi] = term if acc[gi] is None else acc[gi] + term
        row0 = pl.multiple_of(j * CONV_SEG_ROWS, CONV_SEG_ROWS)
        ybuf[s, pl.ds(row0, CONV_SEG_ROWS), :] = jnp.concatenate(acc, axis=0)
        return carry

    lax.fori_loop(0, CONV_SLABS * CONV_SEGS, seg_body, 0, unroll=2)

    y = jnp.concatenate([ybuf[s] for s in range(CONV_SLABS)], axis=1) + b_ref[...]
    mu = jnp.mean(y, axis=-1, keepdims=True)
    yc = y - mu
    var = jnp.mean(yc * yc, axis=-1, keepdims=True)
    z = yc * lax.rsqrt(var + NORM_EPS) * lng_ref[...] + lnb_ref[...]
    o_ref[...] = (z * _sigmoid(z)).astype(BF16)


def _conv_weights(w):
    ws = w.reshape(CONV_KERNEL, CONV_SLABS, V7X_LANES).transpose(1, 0, 2)
    return jnp.broadcast_to(ws[:, :, None, :], (CONV_SLABS, CONV_KERNEL, V7X_SUBLANES, V7X_LANES))


def _conv(glu, w_slab, b, lng, lnb, seq):
    t = glu.shape[0]
    tc = TC_CONV
    ratio = tc // CONV_HALO
    fixed = lambda i: (0, 0)
    return pl.pallas_call(
        functools.partial(_conv_kernel, seq // tc),
        grid=(t // tc,),
        in_specs=[
            pl.BlockSpec((tc, CONV_WIDTH), lambda i: (i, 0)),
            pl.BlockSpec((CONV_HALO, CONV_WIDTH), lambda i: (jnp.maximum(i * ratio - 1, 0), 0)),
            pl.BlockSpec((CONV_SLABS, CONV_KERNEL, V7X_SUBLANES, V7X_LANES), lambda i: (0, 0, 0, 0)),
            pl.BlockSpec((1, CONV_WIDTH), fixed),
            pl.BlockSpec((1, CONV_WIDTH), fixed),
            pl.BlockSpec((1, CONV_WIDTH), fixed),
        ],
        out_specs=pl.BlockSpec((tc, CONV_WIDTH), lambda i: (i, 0)),
        out_shape=jax.ShapeDtypeStruct((t, CONV_WIDTH), BF16),
        scratch_shapes=[pltpu.VMEM((CONV_SLABS, CONV_SEGS * CONV_SEG_LEN, V7X_LANES), F32),
                        pltpu.VMEM((CONV_SLABS, tc, V7X_LANES), F32)],
        compiler_params=_params(("arbitrary",)),
        name="conv",
    )(glu, glu, w_slab, b, lng, lnb)


def _cast_weight_once(w_ref, w_bf16):
    @pl.when(pl.program_id(0) == 0)
    def _():
        w_bf16[...] = w_ref[...].astype(BF16)


def _out_proj_residual(attn_ref, conv_ref, x_ref, w_bf16):
    y = (jnp.dot(attn_ref[...], w_bf16[:ATTN_WIDTH], preferred_element_type=F32)
         + jnp.dot(conv_ref[...], w_bf16[ATTN_WIDTH:], preferred_element_type=F32))
    return x_ref[...] + y


def _mixer_out_router_kernel(attn_ref, conv_ref, x_ref, w_ref, g_ref, wr_ref,
                             o_ref, route_ref, count_ref, w_bf16, tri, carry):
    i = pl.program_id(0)
    tm = x_ref.shape[0]

    @pl.when(i == 0)
    def _():
        carry[...] = jnp.zeros_like(carry)
        earlier = (lax.broadcasted_iota(jnp.int32, (tm, tm), 0)
                   < lax.broadcasted_iota(jnp.int32, (tm, tm), 1))
        tri[...] = jnp.where(earlier, 1.0, 0.0).astype(BF16)

    _cast_weight_once(w_ref, w_bf16)
    x = _out_proj_residual(attn_ref, conv_ref, x_ref, w_bf16)
    o_ref[...] = x

    ms = jnp.mean(x * x, axis=-1, keepdims=True)
    h = x * lax.rsqrt(ms + NORM_EPS) * g_ref[...]
    h_hi = h.astype(BF16)
    h_lo = (h - h_hi.astype(F32)).astype(BF16)
    both = jnp.dot(jnp.concatenate([h_hi, h_lo], axis=1), wr_ref[...],
                   preferred_element_type=F32)
    logits = both[:, :V7X_LANES] + both[:, V7X_LANES:]

    lt = jnp.transpose(logits)[:N_EXPERTS]
    sub = lax.broadcasted_iota(jnp.int32, (N_EXPERTS, tm), 0)
    sub_f = sub.astype(F32)
    neg_inf = -jnp.inf
    none = float(N_EXPERTS)
    m1 = jnp.max(lt, axis=0, keepdims=True)
    i1 = jnp.min(jnp.where(lt == m1, sub_f, none), axis=0, keepdims=True)
    oh1 = sub_f == i1
    lt2 = jnp.where(oh1, neg_inf, lt)
    m2 = jnp.max(lt2, axis=0, keepdims=True)
    i2 = jnp.min(jnp.where(lt2 == m2, sub_f, none), axis=0, keepdims=True)
    oh2 = sub_f == i2
    e = jnp.exp(m2 - m1)
    w1 = 1.0 / (1.0 + e)
    w2 = e / (1.0 + e)

    member = jnp.where(oh1 | oh2, 1.0, 0.0)
    before = (jnp.dot(member.astype(BF16), tri[...], preferred_element_type=F32)
              + carry[:, 0:1])
    rank1 = jnp.sum(jnp.where(oh1, before, 0.0), axis=0, keepdims=True)
    rank2 = jnp.sum(jnp.where(oh2, before, 0.0), axis=0, keepdims=True)
    carry[...] = carry[...] + jnp.sum(member, axis=1, keepdims=True)

    route = jnp.where(sub == 0, i1, 0.0)
    route = jnp.where(sub == 1, i2, route)
    route = jnp.where(sub == 2, w1, route)
    route = jnp.where(sub == 3, w2, route)
    route = jnp.where(sub == 4, rank1, route)
    route = jnp.where(sub == 5, rank2, route)
    route_ref[0] = route
    count_ref[...] = carry[...]


def _mixer_out_router(attn, conv, x, w_out, layer, g, wr_split):
    t = x.shape[0]
    tm = TM_PROJ
    row = lambda i: (i, 0)
    fixed = lambda i: (0, 0)
    in_specs = [
        pl.BlockSpec((tm, ATTN_WIDTH), row),
        pl.BlockSpec((tm, CONV_WIDTH), row),
        pl.BlockSpec((tm, D_MODEL), row),
        pl.BlockSpec((None, D_MODEL, D_MODEL), lambda i: (layer, 0, 0)),
    ]
    w_scratch = pltpu.VMEM((D_MODEL, D_MODEL), BF16)
    return pl.pallas_call(
        _mixer_out_router_kernel,
        grid=(t // tm,),
        in_specs=in_specs + [
            pl.BlockSpec((1, D_MODEL), fixed),
            pl.BlockSpec((2 * D_MODEL, 2 * V7X_LANES), fixed),
        ],
        out_specs=[
            pl.BlockSpec((tm, D_MODEL), row),
            pl.BlockSpec((1, N_EXPERTS, tm), lambda i: (i, 0, 0)),
            pl.BlockSpec((N_EXPERTS, V7X_LANES), fixed),
        ],
        out_shape=[
            jax.ShapeDtypeStruct((t, D_MODEL), F32),
            jax.ShapeDtypeStruct((t // tm, N_EXPERTS, tm), F32),
            jax.ShapeDtypeStruct((N_EXPERTS, V7X_LANES), F32),
        ],
        scratch_shapes=[w_scratch, pltpu.VMEM((tm, tm), BF16),
                        pltpu.VMEM((N_EXPERTS, V7X_LANES), F32)],
        compiler_params=_params(("arbitrary",)),
        name="mixer_out_router",
    )(attn, conv, x, w_out, g, wr_split)


def _ffn_chunk_copies(w_hbm, lead, j, tf, stage, sem, slot):
    cols = pl.ds(j * tf, tf)
    every = slice(None)
    index = (lead + (every, cols), lead + (every, cols), lead + (cols, every))
    return [pltpu.make_async_copy(w_hbm[m].at[index[m]], stage[m].at[slot], sem.at[m, slot])
            for m in range(3)]


def _prefetch_weights(w_hbm, lead, tf, stage, sem):
    for j in range(2):
        for c in _ffn_chunk_copies(w_hbm, lead, j, tf, stage, sem, j):
            c.start()


def _swiglu_chunk(h, res, j):
    gate = jnp.dot(h, res[0][j], preferred_element_type=F32)
    up = jnp.dot(h, res[1][j], preferred_element_type=F32)
    a = (gate * _sigmoid(gate) * up).astype(BF16)
    return jnp.dot(a, res[2][j], preferred_element_type=F32)


def _swiglu_tile(load_weights, prepare, emit, w_hbm, lead, res, stage, sem, hbf, acc, nj, tf,
                 half=None):
    @pl.when(load_weights)
    def _():
        h, ctx = prepare()
        hbf[...] = h
        for j in range(nj):
            slot = j % 2
            for c in _ffn_chunk_copies(w_hbm, lead, j, tf, stage, sem, slot):
                c.wait()
            for m in range(3):
                res[m][j] = stage[m][slot].astype(BF16)
            if j + 2 < nj:
                for c in _ffn_chunk_copies(w_hbm, lead, j + 2, tf, stage, sem, slot):
                    c.start()
            yj = _swiglu_chunk(hbf[...], res, j)
            if j == 0:
                acc[...] = yj
            else:
                acc[...] += yj
        emit(acc[...], ctx)

    def straight_line(make_h, finish):
        h, ctx = make_h()
        y = _swiglu_chunk(h, res, 0)
        for j in range(1, nj):
            y = y + _swiglu_chunk(h, res, j)
        finish(y, ctx)

    resident = jnp.logical_not(load_weights)
    if half is None:
        pl.when(resident)(lambda: straight_line(prepare, emit))
    else:
        is_half, prepare_half, emit_half = half
        pl.when(resident & jnp.logical_not(is_half))(lambda: straight_line(prepare, emit))
        pl.when(resident & is_half)(lambda: straight_line(prepare_half, emit_half))


def _ffn_scratch(tm, nj, tf):
    assert nj >= 2
    return [
        pltpu.VMEM((nj, D_MODEL, tf), BF16),
        pltpu.VMEM((nj, D_MODEL, tf), BF16),
        pltpu.VMEM((nj, tf, D_MODEL), BF16),
        pltpu.VMEM((2, D_MODEL, tf), F32),
        pltpu.VMEM((2, D_MODEL, tf), F32),
        pltpu.VMEM((2, tf, D_MODEL), F32),
        pltpu.SemaphoreType.DMA((3, 2)),
        pltpu.VMEM((tm, D_MODEL), BF16),
        pltpu.VMEM((tm, D_MODEL), F32),
    ]


def _dense_ffn_kernel(ffn_layer, nj, tf, attn_ref, conv_ref, x_ref, w_out_ref, g_ref, wg, wu, wd,
                      o_ref, w_out_bf16, rg, ru, rd, sg, su, sd, sem, hbf, acc):
    first = pl.program_id(0) == 0

    @pl.when(first)
    def _():
        _prefetch_weights((wg, wu, wd), (ffn_layer,), tf, (sg, su, sd), sem)

    _cast_weight_once(w_out_ref, w_out_bf16)

    def prepare():
        x_mid = _out_proj_residual(attn_ref, conv_ref, x_ref, w_out_bf16)
        return _rms_norm_bf16(x_mid, g_ref[...]), x_mid

    def emit(y, x_mid):
        o_ref[...] = x_mid + y

    _swiglu_tile(first, prepare, emit, (wg, wu, wd), (ffn_layer,), (rg, ru, rd),
                 (sg, su, sd), sem, hbf, acc, nj, tf)


def _mixer_out_dense_ffn(attn, conv, x, w_out, layer, g, wg, wu, wd, ffn_layer):
    t = x.shape[0]
    tm = TM_FFN
    tf = TF_DENSE
    nj = wg.shape[2] // tf
    row = lambda i: (i, 0)
    hbm = pl.BlockSpec(memory_space=pl.ANY)
    return pl.pallas_call(
        functools.partial(_dense_ffn_kernel, ffn_layer, nj, tf),
        grid=(t // tm,),
        in_specs=[
            pl.BlockSpec((tm, ATTN_WIDTH), row),
            pl.BlockSpec((tm, CONV_WIDTH), row),
            pl.BlockSpec((tm, D_MODEL), row),
            pl.BlockSpec((None, D_MODEL, D_MODEL), lambda i: (layer, 0, 0)),
            pl.BlockSpec((1, D_MODEL), lambda i: (0, 0)),
            hbm, hbm, hbm,
        ],
        out_specs=pl.BlockSpec((tm, D_MODEL), row),
        out_shape=jax.ShapeDtypeStruct((t, D_MODEL), F32),
        scratch_shapes=[pltpu.VMEM((D_MODEL, D_MODEL), BF16)] + _ffn_scratch(tm, nj, tf),
        compiler_params=_params(("arbitrary",)),
        name="dense_ffn",
    )(attn, conv, x, w_out, g, wg, wu, wd)


def _token_rows(ref, row):
    return ref.at[pl.ds(pl.multiple_of(row * ROW_CHUNKS, ROW_CHUNKS), ROW_CHUNKS), :]


def _dispatch_kernel(fill_ref, pos_ref, x_ref, g_ref, xs_ref, buf, sem, fill_sem):
    tm = x_ref.shape[0]

    @pl.when(pl.program_id(0) == 0)
    def _():
        buf[...] = jnp.zeros_like(buf)
        tile_len = TM_MOE * ROW_CHUNKS

        def fill(n):
            start = pl.multiple_of(fill_ref[n + 1] * tile_len, tile_len)
            return pltpu.make_async_copy(buf.at[pl.ds(0, tile_len), :],
                                         xs_ref.at[pl.ds(start, tile_len), :], fill_sem)

        def start_fill(n, carry):
            fill(n).start()
            return carry

        def wait_fill(n, carry):
            fill(n).wait()
            return carry

        lax.fori_loop(0, fill_ref[0], start_fill, 0)
        lax.fori_loop(0, fill_ref[0], wait_fill, 0)

    x = x_ref[...]
    ms = jnp.mean(x * x, axis=-1, keepdims=True)
    h = x * lax.rsqrt(ms + NORM_EPS) * g_ref[...]
    for c in range(ROW_CHUNKS):
        buf[pl.ds(c, tm, stride=ROW_CHUNKS), :] = h[:, c * V7X_LANES:(c + 1) * V7X_LANES]

    def issue(r, carry):
        src = _token_rows(buf, r)
        for k in range(2):
            dst = _token_rows(xs_ref, pos_ref[0, 0, 2 * r + k])
            pltpu.make_async_copy(src, dst, sem).start(priority=k)
        return carry

    lax.fori_loop(0, tm, issue, 0)
    for k in range(2):
        pltpu.make_async_copy(buf, xs_ref.at[pl.ds(0, tm * ROW_CHUNKS), :], sem).wait()


def _dispatch(fill_tiles, pos_tiles, x, g, n_rows):
    t = x.shape[0]
    tm = TM_TOK
    assert tm >= TM_MOE
    grid_spec = pltpu.PrefetchScalarGridSpec(
        num_scalar_prefetch=1,
        grid=(t // tm,),
        in_specs=[
            pl.BlockSpec((1, 1, 2 * tm), lambda i, fill: (i, 0, 0), memory_space=pltpu.SMEM),
            pl.BlockSpec((tm, D_MODEL), lambda i, fill: (i, 0)),
            pl.BlockSpec((1, D_MODEL), lambda i, fill: (0, 0)),
        ],
        out_specs=pl.BlockSpec(memory_space=pl.ANY),
        scratch_shapes=[pltpu.VMEM((tm * ROW_CHUNKS, V7X_LANES), F32),
                        pltpu.SemaphoreType.DMA(()),
                        pltpu.SemaphoreType.DMA(())],
    )
    return pl.pallas_call(
        _dispatch_kernel,
        grid_spec=grid_spec,
        out_shape=jax.ShapeDtypeStruct((n_rows * ROW_CHUNKS, V7X_LANES), F32),
        compiler_params=_params(("arbitrary",)),
        name="moe_dispatch",
    )(fill_tiles, pos_tiles, x, g)


def _moe_ffn_kernel(layer, nj, tf, te_ref, tr_ref, tl_ref, xs_ref, wg, wu, wd, ys_ref,
                    rg, ru, rd, sg, su, sd, sem, hbf, acc):
    i = pl.program_id(0)
    tm = hbf.shape[0]
    rows = tr_ref[i]
    w_hbm = (wg, wu, wd)
    stage = (sg, su, sd)

    loads = tl_ref[i] != 0
    next_loads = tl_ref[i + 1] != 0

    @pl.when((i == 0) & loads)
    def _():
        _prefetch_weights(w_hbm, (layer, te_ref[0]), tf, stage, sem)

    @pl.when(next_loads & jnp.logical_not(loads))
    def _():
        _prefetch_weights(w_hbm, (layer, te_ref[i + 1]), tf, stage, sem)

    @pl.when(rows > 0)
    def _():
        def load_rows(n):
            x = jnp.concatenate(
                [xs_ref[pl.ds(c, n, stride=ROW_CHUNKS), :] for c in range(ROW_CHUNKS)], axis=1)
            return x.astype(BF16), None

        def store_rows(y, n):
            for c in range(ROW_CHUNKS):
                ys_ref[pl.ds(c, n, stride=ROW_CHUNKS), :] = y[:, c * V7X_LANES:(c + 1) * V7X_LANES]

        half_rows = tm // 2

        def emit_half(y, _):
            store_rows(y, half_rows)
            rest = pl.ds(half_rows * ROW_CHUNKS, (tm - half_rows) * ROW_CHUNKS)
            ys_ref[rest, :] = jnp.zeros(((tm - half_rows) * ROW_CHUNKS, V7X_LANES), F32)

        _swiglu_tile(loads, lambda: load_rows(tm), lambda y, _: store_rows(y, tm),
                     w_hbm, (layer, te_ref[i]), (rg, ru, rd), stage, sem, hbf, acc, nj, tf,
                     half=(rows <= half_rows, lambda: load_rows(half_rows), emit_half))

    @pl.when(rows == 0)
    def _():
        ys_ref[...] = jnp.zeros_like(ys_ref)

    @pl.when(next_loads & loads)
    def _():
        _prefetch_weights(w_hbm, (layer, te_ref[i + 1]), tf, stage, sem)


def _moe_ffn(tile_expert, tile_rows, tile_load, xs, wg, wu, wd, layer):
    n_tiles = tile_rows.shape[0]
    assert tile_expert.shape[0] == n_tiles + 1 and tile_load.shape[0] == n_tiles + 1
    tm = TM_MOE
    tf = TF_MOE
    nj = wg.shape[3] // tf
    hbm = pl.BlockSpec(memory_space=pl.ANY)
    grid_spec = pltpu.PrefetchScalarGridSpec(
        num_scalar_prefetch=3,
        grid=(n_tiles,),
        in_specs=[
            pl.BlockSpec((tm * ROW_CHUNKS, V7X_LANES), lambda i, te, tr, tl: (i, 0)),
            hbm, hbm, hbm,
        ],
        out_specs=pl.BlockSpec((tm * ROW_CHUNKS, V7X_LANES), lambda i, te, tr, tl: (i, 0)),
        scratch_shapes=_ffn_scratch(tm, nj, tf),
    )
    return pl.pallas_call(
        functools.partial(_moe_ffn_kernel, layer, nj, tf),
        grid_spec=grid_spec,
        out_shape=jax.ShapeDtypeStruct(xs.shape, F32),
        compiler_params=_params(("arbitrary",)),
        name="moe_ffn",
    )(tile_expert, tile_rows, tile_load, xs, wg, wu, wd)


def _combine_kernel(pos_ref, x_ref, w_ref, ys_ref, o_ref, y1buf, y2buf, sem):
    tm = x_ref.shape[0]
    half = tm // 2

    def issue(h):
        def body(r, carry):
            for k, ybuf in ((0, y1buf), (1, y2buf)):
                src = _token_rows(ys_ref, pos_ref[0, 0, 2 * r + k])
                pltpu.make_async_copy(src, _token_rows(ybuf, r), sem.at[h]).start(priority=k)
            return carry
        lax.fori_loop(h * half, (h + 1) * half, body, 0, unroll=4)

    issue(0)
    issue(1)
    for h in range(2):
        buf_rows = pl.ds(h * half * ROW_CHUNKS, half * ROW_CHUNKS)
        for ybuf in (y1buf, y2buf):
            pltpu.make_async_copy(ys_ref.at[pl.ds(0, half * ROW_CHUNKS), :], ybuf.at[buf_rows, :],
                                  sem.at[h]).wait()
        tok = slice(h * half, (h + 1) * half)
        w1 = w_ref[tok, 0:1]
        w2 = w_ref[tok, 1:2]
        for c in range(ROW_CHUNKS):
            cols = slice(c * V7X_LANES, (c + 1) * V7X_LANES)
            y1 = y1buf[pl.ds(h * half * ROW_CHUNKS + c, half, stride=ROW_CHUNKS), :]
            y2 = y2buf[pl.ds(h * half * ROW_CHUNKS + c, half, stride=ROW_CHUNKS), :]
            o_ref[tok, cols] = x_ref[tok, cols] + (w1 * y1 + w2 * y2)


def _combine(pos_tiles, x, gate_w, ys):
    t = x.shape[0]
    tm = TM_TOK
    return pl.pallas_call(
        _combine_kernel,
        grid=(t // tm,),
        in_specs=[
            pl.BlockSpec((1, 1, 2 * tm), lambda i: (i, 0, 0), memory_space=pltpu.SMEM),
            pl.BlockSpec((tm, D_MODEL), lambda i: (i, 0)),
            pl.BlockSpec((tm, 2), lambda i: (i, 0)),
            pl.BlockSpec(memory_space=pl.ANY),
        ],
        out_specs=pl.BlockSpec((tm, D_MODEL), lambda i: (i, 0)),
        out_shape=jax.ShapeDtypeStruct((t, D_MODEL), F32),
        scratch_shapes=[pltpu.VMEM((tm * ROW_CHUNKS, V7X_LANES), F32),
                        pltpu.VMEM((tm * ROW_CHUNKS, V7X_LANES), F32),
                        pltpu.SemaphoreType.DMA((2,))],
        compiler_params=_params(("arbitrary",)),
        name="moe_combine",
    )(pos_tiles, x, gate_w, ys)


def _moe_layer(x, route, counts, g, wg, wu, wd, layer):
    t = x.shape[0]
    tm = TM_MOE
    n_tiles = (2 * t) // tm + N_EXPERTS
    per_token = lambda rows: rows.transpose(0, 2, 1).reshape(t, 2)
    expert = per_token(route[:, 0:2]).astype(jnp.int32)
    gate_w = per_token(route[:, 2:4])
    rank = per_token(route[:, 4:6]).astype(jnp.int32)
    count = counts[:, 0].astype(jnp.int32)
    padded = ((count + tm - 1) // tm) * tm
    gend = jnp.cumsum(padded)
    gstart = gend - padded
    ids = jnp.arange(N_EXPERTS, dtype=jnp.int32)
    pos = rank + jnp.sum(jnp.where(expert[..., None] == ids, gstart, 0), axis=-1)
    tile_start = jnp.arange(n_tiles, dtype=jnp.int32) * tm
    last_expert = jnp.max(jnp.where(padded > 0, ids, 0))
    tile_expert = jnp.minimum(
        jnp.sum((tile_start[:, None] >= gend[None, :]).astype(jnp.int32), axis=1), last_expert)
    used_end = (gstart + count)[tile_expert]
    tile_rows = jnp.clip(used_end - tile_start, 0, tm)
    prev_expert = jnp.concatenate([jnp.full((1,), -1, jnp.int32), tile_expert[:-1]])
    tile_load = ((tile_rows > 0) & (tile_expert != prev_expert)).astype(jnp.int32)
    pos_tiles = pos.reshape(t // TM_TOK, 1, 2 * TM_TOK)

    max_fill = n_tiles - (2 * t - N_EXPERTS * (tm - 1)) // tm
    partial = tile_rows < tm
    fill_idx = jnp.nonzero(partial, size=max_fill, fill_value=0)[0].astype(jnp.int32)
    fill_tiles = jnp.concatenate([jnp.sum(partial, dtype=jnp.int32)[None], fill_idx])

    xs = _dispatch(fill_tiles, pos_tiles, x, g, n_tiles * tm)
    one_more = lambda v, last: jnp.concatenate([v, jnp.full((1,), last, jnp.int32)])
    ys = _moe_ffn(one_more(tile_expert, 0), tile_rows, one_more(tile_load, 0), xs,
                  wg, wu, wd, layer)
    return _combine(pos_tiles, x, gate_w, ys)


def kernel(x, attn_norm_g, w_in, q_norm_g, k_norm_g, sinks, conv_w, conv_b, conv_ln_g,
           conv_ln_b, w_out, ffn_norm_g, dense_w_gate, dense_w_up, dense_w_down,
           w_router, moe_w_gate, moe_w_up, moe_w_down):
    batch, seq, d = x.shape
    depth = w_in.shape[0]
    t = batch * seq
    assert d == D_MODEL and seq % TC_CONV == 0 and t % TM_PROJ == 0 and t % TM_TOK == 0
    assert seq % (ATTN_BLOCKS * BLOCK) == 0
    assert w_in.shape[2] == IN_PROJ_WIDTH and conv_w.shape[1] == CONV_KERNEL
    assert dense_w_gate.shape[2] % TF_DENSE == 0 and moe_w_gate.shape[3] % TF_MOE == 0

    xt = x.reshape(t, d)
    head_id = jnp.arange(SEG_WIDTH) // HEAD_DIM
    seg = (head_id[:, None] == head_id[None, :]).astype(BF16)
    scale = HEAD_DIM ** -0.5 * LOG2_E

    for layer in range(depth):
        gain = jnp.concatenate([jnp.tile(q_norm_g[layer], N_Q_HEADS) * scale,
                                jnp.tile(k_norm_g[layer], N_KV_HEADS)])[None, :]
        q, kd, vd, glu = _mixer_in(xt, attn_norm_g[layer][None, :], w_in, layer, gain, seg)
        attn = _attention(q, kd, vd, _attention_bias(sinks[layer]), batch, seq)
        conv = _conv(glu, _conv_weights(conv_w[layer]), conv_b[layer][None, :],
                     conv_ln_g[layer][None, :],
                     conv_ln_b[layer][None, :], seq)
        ffn_g = ffn_norm_g[layer][None, :]
        i = layer // 2
        if layer % 2 == 0:
            xt = _mixer_out_dense_ffn(attn, conv, xt, w_out, layer, ffn_g,
                                      dense_w_gate, dense_w_up, dense_w_down, i)
        else:
            wr = jnp.zeros((D_MODEL, V7X_LANES), F32).at[:, :N_EXPERTS].set(w_router[i])
            wr_hi = wr.astype(BF16)
            wr_lo = (wr - wr_hi.astype(F32)).astype(BF16)
            wr_split = jnp.concatenate(
                [jnp.concatenate([wr_hi, wr_lo], axis=1),
                 jnp.concatenate([wr_hi, jnp.zeros_like(wr_lo)], axis=1)], axis=0)
            xm, route, counts = _mixer_out_router(attn, conv, xt, w_out, layer, ffn_g, wr_split)
            xt = _moe_layer(xm, route, counts, ffn_g, moe_w_gate, moe_w_up, moe_w_down, i)
    return xt.reshape(batch, seq, d)
```
